```python
import jax, jax.numpy as jnp
from jax import lax
import numpy as np

D_MODEL = 1024
BATCH = 8
SEQ = 2048
DEPTH = 4
DEC_BATCH = 32
DEC_SEQ = 1
PAST_LEN = 8192
PAGE_SIZE = 128

N_MIXERS = 3
N_A = (DEPTH + 2) // 3
N_B = (DEPTH + 1) // 3
N_C = DEPTH // 3
NH_A = 8
DV_A = D_MODEL // NH_A
DQK_A = DV_A // 2
NQK_A = NH_A * DQK_A
NV_A = NH_A * DV_A
MLSTM_CHUNK = 64
GATE_CAP = 15.0
H_B = 16
DH_B = D_MODEL // H_B
H_C = 16
DH_C = D_MODEL // H_C
Q_BLOCK = 128
D_FF = ((8 * D_MODEL // 3 + 255) // 256) * 256
EPS = 1e-6

kernel_name = "hybrid_mlstm_fox_stickbreaking_decoder_step"


def _rmsnorm(x, g):
    x32 = x.astype(jnp.float32)
    y = x32 * lax.rsqrt(jnp.mean(x32 * x32, axis=-1, keepdims=True) + EPS) * g.astype(jnp.float32)
    return y.astype(x.dtype)


def _swiglu(h, w_gu, w_down):
    g, u = jnp.split(h @ w_gu, 2, axis=-1)
    return (jax.nn.silu(g) * u) @ w_down


def _gather_pages(pool, page_table):
    g = pool[page_table]
    return g.reshape(g.shape[0], g.shape[1] * g.shape[2], *g.shape[3:])


def _sweep_queries(fn, *q_args):
    T = q_args[0].shape[1]
    if T <= Q_BLOCK or T % Q_BLOCK:
        return fn(*q_args)
    nb = T // Q_BLOCK
    blk = tuple(a.reshape(a.shape[0], nb, Q_BLOCK, *a.shape[2:]).swapaxes(0, 1) for a in q_args)
    out = lax.map(lambda args: fn(*args), blk)
    out = out.swapaxes(0, 1)
    return out.reshape(out.shape[0], T, *out.shape[3:])


def _mlstm_chunkwise(q, k, v, ig, lf, C0, n0, m0):
    B, T, H, _ = q.shape
    L = MLSTM_CHUNK if T % MLSTM_CHUNK == 0 else T
    nc = T // L
    causal = jnp.tril(jnp.ones((L, L), dtype=bool))

    def to_chunks(a):
        return a.reshape(B, nc, L, *a.shape[2:]).swapaxes(0, 1)

    def step(carry, xs):
        C, n, m = carry
        qc, kc, vc, ic, fc = xs
        b = jnp.cumsum(fc, axis=1)
        a = b + m[:, None, :]
        Dm = b[:, :, None, :] - b[:, None, :, :] + ic[:, None, :, :]
        Dm = jnp.where(causal[None, :, :, None], Dm, -jnp.inf)
        mt = jnp.maximum(a, Dm.max(axis=2))
        w_int = jnp.exp(a - mt)
        W = jnp.exp(Dm - mt[:, :, None, :]) * jnp.einsum('bthd,bshd->btsh', qc, kc)
        num = w_int[..., None] * jnp.einsum('bthd,bhde->bthe', qc, C) + jnp.einsum('btsh,bshe->bthe', W, vc)
        den = w_int * jnp.einsum('bthd,bhd->bth', qc, n) + W.sum(axis=2)
        hc = num / jnp.maximum(jnp.abs(den), jnp.exp(-mt))[..., None]
        bL = b[:, -1, :]
        g = bL[:, None, :] - b + ic
        m_new = jnp.maximum(bL + m, g.max(axis=1))
        decay = jnp.exp(bL + m - m_new)
        ws = jnp.exp(g - m_new[:, None, :])
        C_new = decay[..., None, None] * C + jnp.einsum('bsh,bshd,bshe->bhde', ws, kc, vc)
        n_new = decay[..., None] * n + jnp.einsum('bsh,bshd->bhd', ws, kc)
        return (C_new, n_new, m_new), hc

    carry, hs = lax.scan(step, (C0, n0, m0), tuple(to_chunks(a) for a in (q, k, v, ig, lf)))
    hs = hs.swapaxes(0, 1).reshape(B, T, H, hs.shape[-1])
    return hs, carry


def _mlstm_mixer(h, w_in, b_gates, g_norm, w_out, C0, n0, m0):
    B, T, _ = h.shape
    f32 = jnp.float32
    proj = h @ w_in
    q, k, v, o, gates = jnp.split(proj, [NQK_A, 2 * NQK_A, 2 * NQK_A + NV_A, 2 * NQK_A + 2 * NV_A], axis=-1)
    q = q.reshape(B, T, NH_A, DQK_A).astype(f32)
    k = k.reshape(B, T, NH_A, DQK_A).astype(f32) * (DQK_A ** -0.5)
    v = v.reshape(B, T, NH_A, DV_A).astype(f32)
    gates = (gates + b_gates).astype(f32)
    gates = GATE_CAP * jnp.tanh(gates / GATE_CAP)
    ig = gates[..., :NH_A]
    lf = jax.nn.log_sigmoid(gates[..., NH_A:])
    hs, (C, n, m) = _mlstm_chunkwise(q, k, v, ig, lf, C0.astype(f32), n0.astype(f32), m0.astype(f32))
    hs = hs * lax.rsqrt(jnp.mean(hs * hs, axis=-1, keepdims=True) + EPS)
    hs = hs.reshape(B, T, NV_A) * g_norm.astype(f32) * jax.nn.sigmoid(o.astype(f32))
    return hs.astype(h.dtype) @ w_out, C, n, m


def _fox_mixer(h, w_in, b_f, w_out, past_k, past_v, past_logf):
    B, T, _ = h.shape
    f32 = jnp.float32
    HD = H_B * DH_B
    q, k, v, fg = jnp.split(h @ w_in, [HD, 2 * HD, 3 * HD], axis=-1)
    q = q.reshape(B, T, H_B, DH_B)
    k = k.reshape(B, T, H_B, DH_B)
    v = v.reshape(B, T, H_B, DH_B)
    logf = jax.nn.log_sigmoid((fg + b_f).astype(f32))
    if past_k is None:
        K, V, LF = k.astype(f32), v.astype(f32), logf
    else:
        K = jnp.concatenate([past_k.astype(f32), k.astype(f32)], axis=1)
        V = jnp.concatenate([past_v.astype(f32), v.astype(f32)], axis=1)
        LF = jnp.concatenate([past_logf.astype(f32), logf], axis=1)
    Tk = K.shape[1]
    Dcum = jnp.cumsum(LF, axis=1)
    Dk_t = jnp.swapaxes(Dcum, 1, 2)
    Dq = Dcum[:, Tk - T:]
    kpos = jnp.arange(Tk)
    qpos = (Tk - T + jnp.arange(T))[None, :]
    scale = DH_B ** -0.5

    def blk(qb, dqb, qpb):
        s = jnp.einsum('bqhd,bkhd->bhqk', qb, K) * scale
        s = s + (jnp.swapaxes(dqb, 1, 2)[..., None] - Dk_t[:, :, None, :])
        mask = kpos[None, :] <= qpb[0][:, None]
        p = jax.nn.softmax(jnp.where(mask, s, -jnp.inf), axis=-1)
        return jnp.einsum('bhqk,bkhd->bqhd', p, V)

    o = _sweep_queries(blk, q.astype(f32), Dq, qpos)
    y = o.reshape(B, T, HD).astype(h.dtype) @ w_out
    return y, k, v, logf


def _sb_mixer(h, w_in, w_out, past_k, past_v):
    B, T, _ = h.shape
    f32 = jnp.float32
    HD = H_C * DH_C
    q, k, v = jnp.split(h @ w_in, 3, axis=-1)
    q = q.reshape(B, T, H_C, DH_C)
    k = k.reshape(B, T, H_C, DH_C)
    v = v.reshape(B, T, H_C, DH_C)
    if past_k is None:
        K, V = k.astype(f32), v.astype(f32)
    else:
        K = jnp.concatenate([past_k.astype(f32), k.astype(f32)], axis=1)
        V = jnp.concatenate([past_v.astype(f32), v.astype(f32)], axis=1)
    Tk = K.shape[1]
    kpos = jnp.arange(Tk)
    qpos = (Tk - T + jnp.arange(T))[None, :]
    scale = DH_C ** -0.5

    def blk(qb, qpb):
        z = jnp.einsum('bqhd,bkhd->bhqk', qb, K) * scale
        mask = kpos[None, :] < qpb[0][:, None]
        L = jnp.where(mask, jax.nn.log_sigmoid(-z), 0.0)
        S = lax.cumsum(L, axis=3, reverse=True) - L
        A = jnp.where(mask, jnp.exp(jax.nn.log_sigmoid(z) + S), 0.0)
        return jnp.einsum('bhqk,bkhd->bqhd', A, V)

    o = _sweep_queries(blk, q.astype(f32), qpos)
    y = o.reshape(B, T, HD).astype(h.dtype) @ w_out
    return y, k, v


def setup_inputs(seed: int = 0) -> dict:
    key = jax.random.key(seed)
    ks = iter(jax.random.split(key, 64))

    def nrm(shape, scale):
        return jax.random.normal(next(ks), shape, jnp.float32) * scale

    n_pages = PAST_LEN // PAGE_SIZE
    n_pool = (5 * DEC_BATCH * n_pages + 3) // 4
    d = D_MODEL
    a_in = 2 * NQK_A + 2 * NV_A + 2 * NH_A
    b_in = 3 * H_B * DH_B + H_B
    c_in = 3 * H_C * DH_C
    inp = {}
    inp["x_prompt"] = nrm((BATCH, SEQ, d), 1.0)
    inp["x_sample"] = nrm((DEC_BATCH, DEC_SEQ, d), 1.0)
    inp["state_mlstm_c"] = nrm((N_A, DEC_BATCH, NH_A, DQK_A, DV_A), 0.3)
    inp["state_mlstm_n"] = nrm((N_A, DEC_BATCH, NH_A, DQK_A), 0.3)
    inp["state_mlstm_m"] = nrm((N_A, DEC_BATCH, NH_A), 1.0)
    inp["cache_fox_k"] = nrm((N_B, n_pool, PAGE_SIZE, H_B, DH_B), 1.0)
    inp["cache_fox_v"] = nrm((N_B, n_pool, PAGE_SIZE, H_B, DH_B), 1.0)
    inp["cache_fox_logf"] = jax.nn.log_sigmoid(nrm((N_B, n_pool, PAGE_SIZE, H_B), 1.0) + 3.0)
    inp["cache_sb_k"] = nrm((N_C, n_pool, PAGE_SIZE, H_C, DH_C), 1.0)
    inp["cache_sb_v"] = nrm((N_C, n_pool, PAGE_SIZE, H_C, DH_C), 1.0)
    inp["page_table"] = jax.random.permutation(next(ks), n_pool)[:DEC_BATCH * n_pages].reshape(DEC_BATCH, n_pages).astype(jnp.int32)
    inp["norm_mix"] = 1.0 + nrm((DEPTH, d), 0.05)
    inp["norm_ffn"] = 1.0 + nrm((DEPTH, d), 0.05)
    inp["norm_final"] = 1.0 + nrm((d,), 0.05)
    inp["mlstm_w_in"] = nrm((N_A, d, a_in), d ** -0.5)
    ig_b = nrm((N_A, NH_A), 0.1)
    fg_b = jnp.linspace(3.0, 6.0, NH_A, dtype=jnp.float32)[None, :] + nrm((N_A, NH_A), 0.1)
    inp["mlstm_b_gates"] = jnp.concatenate([ig_b, fg_b], axis=-1)
    inp["mlstm_norm"] = 1.0 + nrm((N_A, NV_A), 0.05)
    inp["mlstm_w_out"] = nrm((N_A, NV_A, d), NV_A ** -0.5)
    inp["fox_w_in"] = nrm((N_B, d, b_in), d ** -0.5)
    inp["fox_b_f"] = 2.0 + nrm((N_B, H_B), 0.1)
    inp["fox_w_out"] = nrm((N_B, H_B * DH_B, d), (H_B * DH_B) ** -0.5)
    inp["sb_w_in"] = nrm((N_C, d, c_in), d ** -0.5)
    inp["sb_w_out"] = nrm((N_C, H_C * DH_C, d), (H_C * DH_C) ** -0.5)
    inp["ffn_w_gu"] = nrm((DEPTH, d, 2 * D_FF), d ** -0.5)
    inp["ffn_w_down"] = nrm((DEPTH, D_FF, d), D_FF ** -0.5)
    return inp


def reference(x_prompt, x_sample, state_mlstm_c, state_mlstm_n, state_mlstm_m, cache_fox_k, cache_fox_v, cache_fox_logf, cache_sb_k, cache_sb_v, page_table, norm_mix, norm_ffn, norm_final, mlstm_w_in, mlstm_b_gates, mlstm_norm, mlstm_w_out, fox_w_in, fox_b_f, fox_w_out, sb_w_in, sb_w_out, ffn_w_gu, ffn_w_down):
    f32 = jnp.float32
    xp, xs = x_prompt, x_sample
    B = xp.shape[0]
    pmc, pmn, pmm, smc, smn, smm = [], [], [], [], [], []
    pfk, pfv, pfl, sfk, sfv, sfl = [], [], [], [], [], []
    psk, psv, ssk, ssv = [], [], [], []
    for i in range(DEPTH):
        j = i // N_MIXERS
        hp = _rmsnorm(xp, norm_mix[i])
        hs = _rmsnorm(xs, norm_mix[i])
        if i % N_MIXERS == 0:
            zc = jnp.zeros((B, NH_A, DQK_A, DV_A), f32)
            zn = jnp.zeros((B, NH_A, DQK_A), f32)
            zm = jnp.zeros((B, NH_A), f32)
            yp, c1, n1, m1 = _mlstm_mixer(hp, mlstm_w_in[j], mlstm_b_gates[j], mlstm_norm[j], mlstm_w_out[j], zc, zn, zm)
            ys, c2, n2, m2 = _mlstm_mixer(hs, mlstm_w_in[j], mlstm_b_gates[j], mlstm_norm[j], mlstm_w_out[j], state_mlstm_c[j], state_mlstm_n[j], state_mlstm_m[j])
            pmc.append(c1); pmn.append(n1); pmm.append(m1)
            smc.append(c2); smn.append(n2); smm.append(m2)
        elif i % N_MIXERS == 1:
            yp, k1, v1, l1 = _fox_mixer(hp, fox_w_in[j], fox_b_f[j], fox_w_out[j], None, None, None)
            pk = _gather_pages(cache_fox_k[j], page_table)
            pv = _gather_pages(cache_fox_v[j], page_table)
            pl = _gather_pages(cache_fox_logf[j], page_table)
            ys, k2, v2, l2 = _fox_mixer(hs, fox_w_in[j], fox_b_f[j], fox_w_out[j], pk, pv, pl)
            pfk.append(k1); pfv.append(v1); pfl.append(l1)
            sfk.append(k2); sfv.append(v2); sfl.append(l2)
        else:
            yp, k1, v1 = _sb_mixer(hp, sb_w_in[j], sb_w_out[j], None, None)
            pk = _gather_pages(cache_sb_k[j], page_table)
            pv = _gather_pages(cache_sb_v[j], page_table)
            ys, k2, v2 = _sb_mixer(hs, sb_w_in[j], sb_w_out[j], pk, pv)
            psk.append(k1); psv.append(v1)
            ssk.append(k2); ssv.append(v2)
        xp = xp + yp
        xs = xs + ys
        xp = xp + _swiglu(_rmsnorm(xp, norm_ffn[i]), ffn_w_gu[i], ffn_w_down[i])
        xs = xs + _swiglu(_rmsnorm(xs, norm_ffn[i]), ffn_w_gu[i], ffn_w_down[i])
    y_prompt = _rmsnorm(xp, norm_final)
    y_sample = _rmsnorm(xs, norm_final)
    p_mlstm_c = jnp.stack(pmc); p_mlstm_n = jnp.stack(pmn); p_mlstm_m = jnp.stack(pmm)
    s_mlstm_c = jnp.stack(smc); s_mlstm_n = jnp.stack(smn); s_mlstm_m = jnp.stack(smm)
    p_fox_k = jnp.stack(pfk); p_fox_v = jnp.stack(pfv); p_fox_logf = jnp.stack(pfl)
    s_fox_k = jnp.stack(sfk); s_fox_v = jnp.stack(sfv); s_fox_logf = jnp.stack(sfl)
    p_sb_k = jnp.stack(psk); p_sb_v = jnp.stack(psv)
    s_sb_k = jnp.stack(ssk); s_sb_v = jnp.stack(ssv)
    return (y_prompt, y_sample, p_mlstm_c, p_mlstm_n, p_mlstm_m, s_mlstm_c, s_mlstm_n, s_mlstm_m, p_fox_k, p_fox_v, p_fox_logf, s_fox_k, s_fox_v, s_fox_logf, p_sb_k, p_sb_v, s_sb_k, s_sb_v)
```

```python
import functools

import jax
import jax.numpy as jnp
from jax import lax
from jax.experimental import pallas as pl
from jax.experimental.pallas import tpu as pltpu

F32 = jnp.float32
BF16 = jnp.bfloat16

D_MODEL = 1024
DEPTH = 4
N_MIXERS = 3
NH_A = 8
DV_A = D_MODEL // NH_A
DQK_A = DV_A // 2
NQK_A = NH_A * DQK_A
NV_A = NH_A * DV_A
MLSTM_CHUNK = 64
GATE_CAP = 15.0
N_HEADS = 16
D_HEAD = D_MODEL // N_HEADS
PAGE_SIZE = 128
D_FF = ((8 * D_MODEL // 3 + 255) // 256) * 256
EPS = 1e-6

LANES = 128
MIB = 1024 * 1024

ROW_TILE = 512
FFN_ROW_TILE = 1024
FFN_COL_TILE = 256
ATT_TQ = 512
ATT_TK = 512
SB_TQ = 256
SB_TK = 512
SB_SUB = 128
PAGES_PER_STEP = 8

NT_DIMS = (((1,), (1,)), ((), ()))
TN_DIMS = (((0,), (0,)), ((), ()))


def _params(semantics, vmem_mib):
    return pltpu.CompilerParams(dimension_semantics=semantics, vmem_limit_bytes=vmem_mib * MIB)


def _dot(a, b):
    return jnp.dot(a, b, preferred_element_type=F32)


def _dot_nt(a, b):
    return lax.dot_general(a, b, NT_DIMS, preferred_element_type=F32)


def _dot_exact(a, b):
    return jnp.dot(a, b, precision=lax.Precision.HIGHEST, preferred_element_type=F32)


def _rms(x, g):
    return x * lax.rsqrt(jnp.mean(x * x, axis=-1, keepdims=True) + EPS) * g


def _softplus(x):
    return jnp.maximum(x, 0.0) + jnp.log1p(jnp.exp(-jnp.abs(x)))


def _log_sigmoid(x):
    return -_softplus(-x)


def _sigmoid(x):
    return 1.0 / (1.0 + jnp.exp(-x))


def _iota2(shape, dim):
    return lax.broadcasted_iota(jnp.int32, shape, dim)


def _proj_kernel(x_ref, g_ref, *refs, out_dtypes):
    n_w = len(out_dtypes)
    w_refs, o_refs = refs[:n_w], refs[n_w:]
    h = _rms(x_ref[...], g_ref[...]).astype(BF16)
    k = 0
    for w_ref, dts in zip(w_refs, out_dtypes):
        y = _dot(h, w_ref[...])
        for dt in dts:
            o_refs[k][...] = y.astype(dt)
            k += 1


def _proj(x, g, ws, out_dtypes):
    m = x.shape[0]
    tm = min(m, ROW_TILE)
    in_specs = [pl.BlockSpec((tm, D_MODEL), lambda i: (i, 0)), pl.BlockSpec((1, D_MODEL), lambda i: (0, 0))]
    in_specs += [pl.BlockSpec(w.shape, lambda i: (0, 0)) for w in ws]
    out_shape, out_specs = [], []
    for w, dts in zip(ws, out_dtypes):
        for dt in dts:
            out_shape.append(jax.ShapeDtypeStruct((m, w.shape[1]), dt))
            out_specs.append(pl.BlockSpec((tm, w.shape[1]), lambda i: (i, 0)))
    return pl.pallas_call(
        functools.partial(_proj_kernel, out_dtypes=out_dtypes),
        grid=(m // tm,), in_specs=in_specs, out_specs=out_specs, out_shape=out_shape,
        compiler_params=_params(("parallel",), 48), name="proj",
    )(x, g, *ws)


def _out_ffn_kernel(x_ref, a_ref, wo_ref, g_ref, wg_ref, wu_ref, wd_ref, o_ref, h_sc):
    f = pl.program_id(1)

    @pl.when(f == 0)
    def _():
        xm = x_ref[...] + _dot(a_ref[...].astype(BF16), wo_ref[...])
        o_ref[...] = xm
        h_sc[...] = _rms(xm, g_ref[...]).astype(BF16)

    h = h_sc[...]
    gate = _dot(h, wg_ref[...])
    up = _dot(h, wu_ref[...])
    act = (gate * _sigmoid(gate) * up).astype(BF16)
    o_ref[...] += _dot(act, wd_ref[...])


def _out_ffn(x, a, wo, g, wgu, wd):
    m = x.shape[0]
    tm = min(m, FFN_ROW_TILE)
    nf = D_FF // FFN_COL_TILE
    return pl.pallas_call(
        _out_ffn_kernel,
        grid=(m // tm, nf),
        in_specs=[
            pl.BlockSpec((tm, D_MODEL), lambda i, f: (i, 0)),
            pl.BlockSpec((tm, D_MODEL), lambda i, f: (i, 0)),
            pl.BlockSpec((D_MODEL, D_MODEL), lambda i, f: (0, 0)),
            pl.BlockSpec((1, D_MODEL), lambda i, f: (0, 0)),
            pl.BlockSpec((D_MODEL, FFN_COL_TILE), lambda i, f: (0, f)),
            pl.BlockSpec((D_MODEL, FFN_COL_TILE), lambda i, f: (0, nf + f)),
            pl.BlockSpec((FFN_COL_TILE, D_MODEL), lambda i, f: (f, 0)),
        ],
        out_specs=pl.BlockSpec((tm, D_MODEL), lambda i, f: (i, 0)),
        out_shape=jax.ShapeDtypeStruct((m, D_MODEL), F32),
        scratch_shapes=[pltpu.VMEM((tm, D_MODEL), BF16)],
        compiler_params=_params(("parallel", "arbitrary"), 48), name="out_ffn",
    )(x, a, wo, g, wgu, wgu, wd)


def _final_norm_kernel(x_ref, g_ref, o_ref):
    o_ref[...] = _rms(x_ref[...], g_ref[...])


def _final_norm(x, g):
    m = x.shape[0]
    tm = min(m, FFN_ROW_TILE)
    return pl.pallas_call(
        _final_norm_kernel, grid=(m // tm,),
        in_specs=[pl.BlockSpec((tm, D_MODEL), lambda i: (i, 0)), pl.BlockSpec((1, D_MODEL), lambda i: (0, 0))],
        out_specs=pl.BlockSpec((tm, D_MODEL), lambda i: (i, 0)),
        out_shape=jax.ShapeDtypeStruct((m, D_MODEL), F32),
        compiler_params=_params(("parallel",), 32), name="final_norm",
    )(x, g)


def _mlstm_gates(pre, bias):
    gates = GATE_CAP * jnp.tanh((pre + bias) / GATE_CAP)
    return gates[:, :NH_A], _log_sigmoid(gates[:, NH_A:])


def _mlstm_head_out(hc, gn, o):
    hn = hc * lax.rsqrt(jnp.mean(hc * hc, axis=-1, keepdims=True) + EPS)
    return hn * gn * _sigmoid(o)


def _mlstm_prompt_kernel(q_ref, k_ref, v_ref, o_ref, gt_ref, bg_ref, gn_ref, h_ref, c_ref, n_ref, m_ref):
    L = MLSTM_CHUNK

    @pl.when(pl.program_id(1) == 0)
    def _():
        c_ref[...] = jnp.zeros_like(c_ref)
        n_ref[...] = jnp.zeros_like(n_ref)
        m_ref[...] = jnp.zeros_like(m_ref)

    ig, lf = _mlstm_gates(gt_ref[...], bg_ref[...])
    row, col = _iota2((L, L), 0), _iota2((L, L), 1)
    causal = row >= col
    bcum = _dot_exact(causal.astype(F32), lf)
    r_t = (ig - bcum).T
    scale = DQK_A ** -0.5
    for h in range(NH_A):
        qs, vs = slice(h * DQK_A, (h + 1) * DQK_A), slice(h * DV_A, (h + 1) * DV_A)
        qh, kh, vh = q_ref[:, qs], k_ref[:, qs], v_ref[:, vs]
        c_old = c_ref[0, h]
        n_old = n_ref[0, h:h + 1, :]
        m_old = m_ref[0, :, h:h + 1]
        b_col = bcum[:, h:h + 1]
        a_col = b_col + m_old
        dm = jnp.where(causal, b_col + r_t[h:h + 1, :], -jnp.inf)
        mt = jnp.maximum(a_col, jnp.max(dm, axis=1, keepdims=True))
        w_int = jnp.exp(a_col - mt)
        w = jnp.exp(dm - mt) * (_dot_nt(qh, kh) * scale)
        num = w_int * _dot(qh, c_old.astype(BF16)) + _dot(w.astype(BF16), vh)
        den = w_int * jnp.sum(qh.astype(F32) * n_old, axis=1, keepdims=True) + jnp.sum(w, axis=1, keepdims=True)
        hc = num / jnp.maximum(jnp.abs(den), jnp.exp(-mt))
        h_ref[:, vs] = _mlstm_head_out(hc, gn_ref[:, vs], o_ref[:, vs]).astype(h_ref.dtype)
        b_last = b_col[L - 1:L, :]
        g_col = b_last - b_col + ig[:, h:h + 1]
        m_new = jnp.maximum(b_last + m_old, jnp.max(g_col, axis=0, keepdims=True))
        decay = jnp.exp(b_last + m_old - m_new)
        kw = kh.astype(F32) * (jnp.exp(g_col - m_new) * scale)
        c_ref[0, h] = decay * c_old + lax.dot_general(kw.astype(BF16), vh, TN_DIMS, preferred_element_type=F32)
        n_ref[0, h:h + 1, :] = decay * n_old + jnp.sum(kw, axis=0, keepdims=True)
        m_ref[0, :, h:h + 1] = m_new


def _mlstm_prompt(q, k, v, o, gates, b_gates, g_norm, batch):
    m = q.shape[0]
    nc = m // batch // MLSTM_CHUNK
    L = MLSTM_CHUNK
    rows = lambda b, c: (b * nc + c, 0)
    fixed = lambda b, c: (0, 0)
    return pl.pallas_call(
        _mlstm_prompt_kernel,
        grid=(batch, nc),
        in_specs=[
            pl.BlockSpec((L, NQK_A), rows), pl.BlockSpec((L, NQK_A), rows), pl.BlockSpec((L, NV_A), rows),
            pl.BlockSpec((L, NV_A), rows), pl.BlockSpec((L, 2 * NH_A), rows),
            pl.BlockSpec((1, 2 * NH_A), fixed), pl.BlockSpec((1, NV_A), fixed),
        ],
        out_specs=[
            pl.BlockSpec((L, NV_A), rows),
            pl.BlockSpec((1, NH_A, DQK_A, DV_A), lambda b, c: (b, 0, 0, 0)),
            pl.BlockSpec((1, NH_A, DQK_A), lambda b, c: (b, 0, 0)),
            pl.BlockSpec((1, 1, NH_A), lambda b, c: (b, 0, 0)),
        ],
        out_shape=[
            jax.ShapeDtypeStruct((m, NV_A), BF16),
            jax.ShapeDtypeStruct((batch, NH_A, DQK_A, DV_A), F32),
            jax.ShapeDtypeStruct((batch, NH_A, DQK_A), F32),
            jax.ShapeDtypeStruct((batch, 1, NH_A), F32),
        ],
        compiler_params=_params(("parallel", "arbitrary"), 32), name="mlstm_prompt",
    )(q, k, v, o, gates, b_gates, g_norm)


MLSTM_DEC_TILE = 8


def _mlstm_decode_kernel(q_ref, k_ref, v_ref, o_ref, gt_ref, bg_ref, gn_ref, c0_ref, n0_ref, m0_ref,
                         h_ref, c_ref, n_ref, m_ref):
    ig, lf = _mlstm_gates(gt_ref[...], bg_ref[...])
    scale = DQK_A ** -0.5
    q_t, k_t = q_ref[...].T, k_ref[...].T
    for j in range(MLSTM_DEC_TILE):
        for h in range(NH_A):
            qs, vs = slice(h * DQK_A, (h + 1) * DQK_A), slice(h * DV_A, (h + 1) * DV_A)
            q_row, k_row = q_ref[j:j + 1, qs], k_ref[j:j + 1, qs] * scale
            q_col, k_col = q_t[qs, j:j + 1], k_t[qs, j:j + 1] * scale
            v_row = v_ref[j:j + 1, vs]
            c_old, n_old, m_old = c0_ref[j, h], n0_ref[j, h:h + 1, :], m0_ref[j:j + 1, h:h + 1]
            i_g, a = ig[j:j + 1, h:h + 1], lf[j:j + 1, h:h + 1] + m_old
            mt = jnp.maximum(a, i_g)
            w_int = jnp.exp(a - mt)
            w = jnp.exp(i_g - mt) * jnp.sum(q_row * k_row, axis=1, keepdims=True)
            num = w_int * jnp.sum(q_col * c_old, axis=0, keepdims=True) + w * v_row
            den = w_int * jnp.sum(q_row * n_old, axis=1, keepdims=True) + w
            hc = num / jnp.maximum(jnp.abs(den), jnp.exp(-mt))
            h_ref[j:j + 1, vs] = _mlstm_head_out(hc, gn_ref[:, vs], o_ref[j:j + 1, vs])
            m_new = jnp.maximum(a, i_g)
            decay = jnp.exp(a - m_new)
            ws = jnp.exp(i_g - m_new)
            c_ref[j, h] = decay * c_old + (ws * k_col) * v_row
            n_ref[j, h:h + 1, :] = decay * n_old + ws * k_row
            m_ref[j:j + 1, h:h + 1] = m_new


def _mlstm_decode(q, k, v, o, gates, b_gates, g_norm, c0, n0, m0):
    nb = q.shape[0]
    tb = MLSTM_DEC_TILE
    rows = lambda i: (i, 0)
    fixed = lambda i: (0, 0)
    return pl.pallas_call(
        _mlstm_decode_kernel,
        grid=(nb // tb,),
        in_specs=[
            pl.BlockSpec((tb, NQK_A), rows), pl.BlockSpec((tb, NQK_A), rows), pl.BlockSpec((tb, NV_A), rows),
            pl.BlockSpec((tb, NV_A), rows), pl.BlockSpec((tb, 2 * NH_A), rows),
            pl.BlockSpec((1, 2 * NH_A), fixed), pl.BlockSpec((1, NV_A), fixed),
            pl.BlockSpec((tb, NH_A, DQK_A, DV_A), lambda i: (i, 0, 0, 0)),
            pl.BlockSpec((tb, NH_A, DQK_A), lambda i: (i, 0, 0)),
            pl.BlockSpec((tb, NH_A), rows),
        ],
        out_specs=[
            pl.BlockSpec((tb, NV_A), rows),
            pl.BlockSpec((tb, NH_A, DQK_A, DV_A), lambda i: (i, 0, 0, 0)),
            pl.BlockSpec((tb, NH_A, DQK_A), lambda i: (i, 0, 0)),
            pl.BlockSpec((tb, NH_A), rows),
        ],
        out_shape=[
            jax.ShapeDtypeStruct((nb, NV_A), F32),
            jax.ShapeDtypeStruct(c0.shape, F32),
            jax.ShapeDtypeStruct(n0.shape, F32),
            jax.ShapeDtypeStruct(m0.shape, F32),
        ],
        compiler_params=_params(("parallel",), 32), name="mlstm_decode",
    )(q, k, v, o, gates, b_gates, g_norm, c0, n0, m0)


def _fox_gates_kernel(fg_ref, bf_ref, lf_ref, d_ref, dt_ref):
    t = fg_ref.shape[0]
    lf = _log_sigmoid(fg_ref[...] + bf_ref[...])
    lf_ref[...] = lf
    tril = (_iota2((LANES, LANES), 0) >= _iota2((LANES, LANES), 1)).astype(F32)
    carry = jnp.zeros((1, N_HEADS), F32)
    for blk in range(t // LANES):
        rows = slice(blk * LANES, (blk + 1) * LANES)
        d = _dot_exact(tril, lf[rows]) + carry
        d_ref[rows, :] = d
        carry = d[LANES - 1:LANES, :]
    dt_ref[...] = d_ref[...].T


def _fox_gates(fg, b_f, batch):
    m = fg.shape[0]
    t = m // batch
    return pl.pallas_call(
        _fox_gates_kernel,
        grid=(batch,),
        in_specs=[pl.BlockSpec((t, N_HEADS), lambda b: (b, 0)), pl.BlockSpec((1, N_HEADS), lambda b: (0, 0))],
        out_specs=[pl.BlockSpec((t, N_HEADS), lambda b: (b, 0)), pl.BlockSpec((t, N_HEADS), lambda b: (b, 0)),
                   pl.BlockSpec((N_HEADS, t), lambda b: (0, b))],
        out_shape=[jax.ShapeDtypeStruct((m, N_HEADS), F32), jax.ShapeDtypeStruct((m, N_HEADS), F32),
                   jax.ShapeDtypeStruct((N_HEADS, m), F32)],
        compiler_params=_params(("parallel",), 32), name="fox_gates",
    )(fg, b_f)


HEADS_PER_STEP = LANES // D_HEAD


def _fox_prompt_kernel(q_ref, k_ref, v_ref, d_ref, dt_ref, o_ref, m_sc, l_sc, acc_sc):
    hp, qi, ki = pl.program_id(1), pl.program_id(2), pl.program_id(3)
    tq, tk = q_ref.shape[0], k_ref.shape[0]

    @pl.when(ki == 0)
    def _():
        m_sc[...] = jnp.full_like(m_sc, -jnp.inf)
        l_sc[...] = jnp.zeros_like(l_sc)
        acc_sc[...] = jnp.zeros_like(acc_sc)

    @pl.when(ki <= qi)
    def _():
        visible = (ki * tk + _iota2((tq, tk), 1)) <= (qi * tq + _iota2((tq, tk), 0))
        head_lane = _iota2((tq, N_HEADS), 1)
        for hh in range(HEADS_PER_STEP):
            head = hp * HEADS_PER_STEP + hh
            cs = slice(hh * D_HEAD, (hh + 1) * D_HEAD)
            dq = jnp.sum(jnp.where(head_lane == head, d_ref[...], 0.0), axis=1, keepdims=True)
            dk = dt_ref[pl.ds(head, 1), :]
            s = _dot_nt(q_ref[:, cs], k_ref[:, cs]) * (D_HEAD ** -0.5) + (dq - dk)
            s = jnp.where(visible, s, -jnp.inf)
            m_old = m_sc[hh]
            m_new = jnp.maximum(m_old, jnp.max(s, axis=1, keepdims=True))
            alpha = jnp.exp(m_old - m_new)
            p = jnp.exp(s - m_new)
            l_sc[hh] = alpha * l_sc[hh] + jnp.sum(p, axis=1, keepdims=True)
            acc_sc[hh] = alpha * acc_sc[hh] + _dot(p.astype(BF16), v_ref[:, cs])
            m_sc[hh] = m_new

    @pl.when(ki == pl.num_programs(3) - 1)
    def _():
        for hh in range(HEADS_PER_STEP):
            o_ref[:, hh * D_HEAD:(hh + 1) * D_HEAD] = (acc_sc[hh] / l_sc[hh]).astype(o_ref.dtype)


def _fox_prompt(q, k, v, d, dt, batch):
    m = q.shape[0]
    t = m // batch
    tq, tk = ATT_TQ, ATT_TK
    nq, nk = t // tq, t // tk
    qmap = lambda b, hp, qi, ki: (b * nq + qi, hp)
    kmap = lambda b, hp, qi, ki: (b * nk + jnp.minimum(ki, qi), hp)
    return pl.pallas_call(
        _fox_prompt_kernel,
        grid=(batch, N_HEADS // HEADS_PER_STEP, nq, nk),
        in_specs=[
            pl.BlockSpec((tq, LANES), qmap), pl.BlockSpec((tk, LANES), kmap), pl.BlockSpec((tk, LANES), kmap),
            pl.BlockSpec((tq, N_HEADS), lambda b, hp, qi, ki: (b * nq + qi, 0)),
            pl.BlockSpec((N_HEADS, tk), lambda b, hp, qi, ki: (0, b * nk + jnp.minimum(ki, qi))),
        ],
        out_specs=pl.BlockSpec((tq, LANES), qmap),
        out_shape=jax.ShapeDtypeStruct((m, D_MODEL), BF16),
        scratch_shapes=[pltpu.VMEM((HEADS_PER_STEP, tq, 1), F32), pltpu.VMEM((HEADS_PER_STEP, tq, 1), F32),
                        pltpu.VMEM((HEADS_PER_STEP, tq, D_HEAD), F32)],
        compiler_params=_params(("parallel", "parallel", "parallel", "arbitrary"), 32), name="fox_prompt",
    )(q, k, v, d, dt)


def _suffix_matrix(n):
    return (_iota2((n, n), 0) > _iota2((n, n), 1)).astype(BF16)


def _split_dot(x, u):
    hi = x.astype(BF16)
    lo = (x - hi.astype(F32)).astype(BF16)
    return _dot(hi, u) + _dot(lo, u)


def _sb_prompt_kernel(q_ref, k_ref, v_ref, o_ref, carry_sc, acc_sc):
    qi, kk = pl.program_id(2), pl.program_id(3)
    tq, tk = q_ref.shape[0], k_ref.shape[0]
    n_sub = tk // SB_SUB

    @pl.when(kk == 0)
    def _():
        carry_sc[...] = jnp.zeros_like(carry_sc)
        acc_sc[...] = jnp.zeros_like(acc_sc)

    kb = (qi * tq + tq - 1) // tk - kk

    @pl.when(kb >= 0)
    def _():
        u = _suffix_matrix(SB_SUB)
        qpos = qi * tq + _iota2((tq, SB_SUB), 0)
        for hh in range(HEADS_PER_STEP):
            cs = slice(hh * D_HEAD, (hh + 1) * D_HEAD)
            qh = q_ref[:, cs]
            carry = carry_sc[hh]
            acc = acc_sc[hh]
            for sub in reversed(range(n_sub)):
                rows = slice(sub * SB_SUB, (sub + 1) * SB_SUB)
                z = _dot_nt(qh, k_ref[rows, cs]) * (D_HEAD ** -0.5)
                visible = (kb * tk + sub * SB_SUB + _iota2((tq, SB_SUB), 1)) < qpos
                log_1mb = jnp.where(visible, -_softplus(z), 0.0)
                s = _split_dot(log_1mb, u) + carry
                carry = carry + jnp.sum(log_1mb, axis=1, keepdims=True)
                a = jnp.where(visible, jnp.exp(z + log_1mb + s), 0.0)
                acc = acc + _dot(a.astype(BF16), v_ref[rows, cs])
            carry_sc[hh] = carry
            acc_sc[hh] = acc

    @pl.when(kk == pl.num_programs(3) - 1)
    def _():
        for hh in range(HEADS_PER_STEP):
            o_ref[:, hh * D_HEAD:(hh + 1) * D_HEAD] = acc_sc[hh].astype(o_ref.dtype)


def _sb_prompt(q, k, v, batch):
    m = q.shape[0]
    t = m // batch
    tq, tk = SB_TQ, SB_TK
    nq, nk = t // tq, t // tk
    qmap = lambda b, hp, qi, kk: (b * nq + qi, hp)
    kmap = lambda b, hp, qi, kk: (b * nk + jnp.maximum((qi * tq + tq - 1) // tk - kk, 0), hp)
    return pl.pallas_call(
        _sb_prompt_kernel,
        grid=(batch, N_HEADS // HEADS_PER_STEP, nq, nk),
        in_specs=[pl.BlockSpec((tq, LANES), qmap), pl.BlockSpec((tk, LANES), kmap), pl.BlockSpec((tk, LANES), kmap)],
        out_specs=pl.BlockSpec((tq, LANES), qmap),
        out_shape=jax.ShapeDtypeStruct((m, D_MODEL), BF16),
        scratch_shapes=[pltpu.VMEM((HEADS_PER_STEP, tq, 1), F32), pltpu.VMEM((HEADS_PER_STEP, tq, D_HEAD), F32)],
        compiler_params=_params(("parallel", "parallel", "parallel", "arbitrary"), 32), name="sb_prompt",
    )(q, k, v)


def _block_diag_mask():
    return _iota2((N_HEADS, D_MODEL), 1) // D_HEAD == _iota2((N_HEADS, D_MODEL), 0)


def _page_spec(block, n_pages, slot):
    p = PAGES_PER_STEP
    return pl.BlockSpec(block, lambda b, j, pt: (pt[b, n_pages - 1 - (j * p + slot)],) + (0,) * (len(block) - 1))


def _fox_decode_kernel(pt_ref, q_ref, kn_ref, vn_ref, lfn_ref, *refs):
    p = PAGES_PER_STEP
    k_refs, v_refs, lf_refs = refs[:p], refs[p:2 * p], refs[2 * p:3 * p]
    o_ref, qbd_sc, m_sc, l_sc, carry_sc, acc_sc = refs[3 * p:]
    j = pl.program_id(1)

    @pl.when(j == 0)
    def _():
        qbd = jnp.where(_block_diag_mask(), q_ref[0] * (D_HEAD ** -0.5), 0.0)
        qbd_sc[...] = qbd.astype(BF16)
        m_sc[...] = jnp.sum(qbd * kn_ref[0], axis=1, keepdims=True)
        l_sc[...] = jnp.ones_like(l_sc)
        acc_sc[...] = jnp.broadcast_to(vn_ref[0], acc_sc.shape)
        carry_sc[...] = lfn_ref[0].T

    u = (_iota2((PAGE_SIZE, PAGE_SIZE), 0) > _iota2((PAGE_SIZE, PAGE_SIZE), 1)).astype(F32)
    qbd = qbd_sc[...]
    m_run, l_run, carry, acc = m_sc[...], l_sc[...], carry_sc[...], acc_sc[...]
    for i in range(p):
        lf_t = lf_refs[i][0].T
        bias = _dot_exact(lf_t, u) + carry
        carry = carry + jnp.sum(lf_t, axis=1, keepdims=True)
        s = _dot_nt(qbd, k_refs[i][0].astype(BF16)) + bias
        m_new = jnp.maximum(m_run, jnp.max(s, axis=1, keepdims=True))
        alpha = jnp.exp(m_run - m_new)
        pr = jnp.exp(s - m_new)
        l_run = alpha * l_run + jnp.sum(pr, axis=1, keepdims=True)
        acc = alpha * acc + _dot(pr.astype(BF16), v_refs[i][0].astype(BF16))
        m_run = m_new
    m_sc[...], l_sc[...], carry_sc[...], acc_sc[...] = m_run, l_run, carry, acc

    @pl.when(j == pl.num_programs(1) - 1)
    def _():
        o_ref[0] = jnp.sum(jnp.where(_block_diag_mask(), acc / l_run, 0.0), axis=0, keepdims=True)


def _fox_decode(q, k_new, v_new, lf_new, cache_k, cache_v, cache_lf, page_table):
    nb, n_pages = page_table.shape
    p = PAGES_PER_STEP
    row = lambda b, j, pt: (b, 0, 0)
    in_specs = [pl.BlockSpec((1, 1, D_MODEL), row)] * 3 + [pl.BlockSpec((1, 1, N_HEADS), row)]
    in_specs += [_page_spec((1, PAGE_SIZE, D_MODEL), n_pages, i) for i in range(p)] * 2
    in_specs += [_page_spec((1, PAGE_SIZE, N_HEADS), n_pages, i) for i in range(p)]
    grid_spec = pltpu.PrefetchScalarGridSpec(
        num_scalar_prefetch=1, grid=(nb, n_pages // p), in_specs=in_specs,
        out_specs=pl.BlockSpec((1, 1, D_MODEL), row),
        scratch_shapes=[pltpu.VMEM((N_HEADS, D_MODEL), BF16), pltpu.VMEM((N_HEADS, 1), F32),
                        pltpu.VMEM((N_HEADS, 1), F32), pltpu.VMEM((N_HEADS, 1), F32),
                        pltpu.VMEM((N_HEADS, D_MODEL), F32)])
    return pl.pallas_call(
        _fox_decode_kernel, grid_spec=grid_spec,
        out_shape=jax.ShapeDtypeStruct((nb, 1, D_MODEL), F32),
        compiler_params=_params(("parallel", "arbitrary"), 40), name="fox_decode",
    )(page_table, q, k_new, v_new, lf_new, *([cache_k] * p), *([cache_v] * p), *([cache_lf] * p))


def _sb_decode_kernel(pt_ref, q_ref, *refs):
    p = PAGES_PER_STEP
    k_refs, v_refs = refs[:p], refs[p:2 * p]
    o_ref, qbd_sc, carry_sc, acc_sc = refs[2 * p:]
    j = pl.program_id(1)

    @pl.when(j == 0)
    def _():
        qbd_sc[...] = jnp.where(_block_diag_mask(), q_ref[0] * (D_HEAD ** -0.5), 0.0).astype(BF16)
        carry_sc[...] = jnp.zeros_like(carry_sc)
        acc_sc[...] = jnp.zeros_like(acc_sc)

    u = (_iota2((PAGE_SIZE, PAGE_SIZE), 0) > _iota2((PAGE_SIZE, PAGE_SIZE), 1)).astype(F32)
    qbd = qbd_sc[...]
    carry, acc = carry_sc[...], acc_sc[...]
    for i in range(p):
        z = _dot_nt(qbd, k_refs[i][0].astype(BF16))
        log_1mb = -_softplus(z)
        s = _dot_exact(log_1mb, u) + carry
        carry = carry + jnp.sum(log_1mb, axis=1, keepdims=True)
        a = jnp.exp(z + log_1mb + s)
        acc = acc + _dot(a.astype(BF16), v_refs[i][0].astype(BF16))
    carry_sc[...], acc_sc[...] = carry, acc

    @pl.when(j == pl.num_programs(1) - 1)
    def _():
        o_ref[0] = jnp.sum(jnp.where(_block_diag_mask(), acc, 0.0), axis=0, keepdims=True)


def _sb_decode(q, cache_k, cache_v, page_table):
    nb, n_pages = page_table.shape
    p = PAGES_PER_STEP
    row = lambda b, j, pt: (b, 0, 0)
    in_specs = [pl.BlockSpec((1, 1, D_MODEL), row)]
    in_specs += [_page_spec((1, PAGE_SIZE, D_MODEL), n_pages, i) for i in range(p)] * 2
    grid_spec = pltpu.PrefetchScalarGridSpec(
        num_scalar_prefetch=1, grid=(nb, n_pages // p), in_specs=in_specs,
        out_specs=pl.BlockSpec((1, 1, D_MODEL), row),
        scratch_shapes=[pltpu.VMEM((N_HEADS, D_MODEL), BF16), pltpu.VMEM((N_HEADS, 1), F32),
                        pltpu.VMEM((N_HEADS, D_MODEL), F32)])
    return pl.pallas_call(
        _sb_decode_kernel, grid_spec=grid_spec,
        out_shape=jax.ShapeDtypeStruct((nb, 1, D_MODEL), F32),
        compiler_params=_params(("parallel", "arbitrary"), 40), name="sb_decode",
    )(page_table, q, *([cache_k] * p), *([cache_v] * p))


def _fox_logf_kernel(fg_ref, bf_ref, lf_ref):
    lf_ref[...] = _log_sigmoid(fg_ref[...] + bf_ref[...])


def _fox_logf(fg, b_f):
    return pl.pallas_call(_fox_logf_kernel, out_shape=jax.ShapeDtypeStruct(fg.shape, F32), name="fox_logf")(fg, b_f)


def kernel(x_prompt, x_sample, state_mlstm_c, state_mlstm_n, state_mlstm_m, cache_fox_k, cache_fox_v, cache_fox_logf, cache_sb_k, cache_sb_v, page_table, norm_mix, norm_ffn, norm_final, mlstm_w_in, mlstm_b_gates, mlstm_norm, mlstm_w_out, fox_w_in, fox_b_f, fox_w_out, sb_w_in, sb_w_out, ffn_w_gu, ffn_w_down):
    batch, seq, d = x_prompt.shape
    nb = x_sample.shape[0]
    n_pool = cache_fox_k.shape[1]
    xp = x_prompt.reshape(batch * seq, d)
    xs = x_sample.reshape(nb, d)
    out = {name: [] for name in ("pmc", "pmn", "pmm", "smc", "smn", "smm", "pfk", "pfv", "pfl", "sfk", "sfv", "sfl",
                                 "psk", "psv", "ssk", "ssv")}
    for i in range(DEPTH):
        j = i // N_MIXERS
        g_mix = norm_mix[i].reshape(1, d)
        if i % N_MIXERS == 0:
            w_in = mlstm_w_in[j].astype(BF16)
            splits = (NQK_A, 2 * NQK_A, 2 * NQK_A + NV_A, 2 * NQK_A + 2 * NV_A)
            ws = [w_in[:, a:b] for a, b in zip((0,) + splits, splits + (w_in.shape[1],))]
            b_gates = mlstm_b_gates[j].reshape(1, 2 * NH_A)
            g_norm = mlstm_norm[j].reshape(1, NV_A)
            w_out = mlstm_w_out[j].astype(BF16)
            q, k, v, o, gt = _proj(xp, g_mix, ws, ((BF16,), (BF16,), (BF16,), (F32,), (F32,)))
            ap, c1, n1, m1 = _mlstm_prompt(q, k, v, o, gt, b_gates, g_norm, batch)
            q, k, v, o, gt = _proj(xs, g_mix, ws, ((F32,),) * 5)
            a_s, c2, n2, m2 = _mlstm_decode(q, k, v, o, gt, b_gates, g_norm,
                                            state_mlstm_c[j], state_mlstm_n[j], state_mlstm_m[j])
            out["pmc"].append(c1); out["pmn"].append(n1); out["pmm"].append(m1.reshape(batch, NH_A))
            out["smc"].append(c2); out["smn"].append(n2); out["smm"].append(m2)
        elif i % N_MIXERS == 1:
            w_in = fox_w_in[j].astype(BF16)
            ws = [w_in[:, a:b] for a, b in ((0, d), (d, 2 * d), (2 * d, 3 * d), (3 * d, 3 * d + N_HEADS))]
            b_f = fox_b_f[j].reshape(1, N_HEADS)
            w_out = fox_w_out[j].astype(BF16)
            q, k1, kb, v1, vb, fg = _proj(xp, g_mix, ws, ((BF16,), (F32, BF16), (F32, BF16), (F32,)))
            l1, dcum, dcum_t = _fox_gates(fg, b_f, batch)
            ap = _fox_prompt(q, kb, vb, dcum, dcum_t, batch)
            q, k2, v2, fg = _proj(xs, g_mix, ws, ((F32,),) * 4)
            l2 = _fox_logf(fg, b_f)
            a_s = _fox_decode(q.reshape(nb, 1, d), k2.reshape(nb, 1, d), v2.reshape(nb, 1, d),
                              l2.reshape(nb, 1, N_HEADS),
                              cache_fox_k[j].reshape(n_pool, PAGE_SIZE, d), cache_fox_v[j].reshape(n_pool, PAGE_SIZE, d),
                              cache_fox_logf[j], page_table).reshape(nb, d)
            out["pfk"].append(k1.reshape(batch, seq, N_HEADS, D_HEAD)); out["pfv"].append(v1.reshape(batch, seq, N_HEADS, D_HEAD))
            out["pfl"].append(l1.reshape(batch, seq, N_HEADS))
            out["sfk"].append(k2.reshape(nb, 1, N_HEADS, D_HEAD)); out["sfv"].append(v2.reshape(nb, 1, N_HEADS, D_HEAD))
            out["sfl"].append(l2.reshape(nb, 1, N_HEADS))
        else:
            w_in = sb_w_in[j].astype(BF16)
            ws = [w_in[:, a:b] for a, b in ((0, d), (d, 2 * d), (2 * d, 3 * d))]
            w_out = sb_w_out[j].astype(BF16)
            q, k1, kb, v1, vb = _proj(xp, g_mix, ws, ((BF16,), (F32, BF16), (F32, BF16)))
            ap = _sb_prompt(q, kb, vb, batch)
            q, k2, v2 = _proj(xs, g_mix, ws, ((F32,),) * 3)
            a_s = _sb_decode(q.reshape(nb, 1, d), cache_sb_k[j].reshape(n_pool, PAGE_SIZE, d),
                             cache_sb_v[j].reshape(n_pool, PAGE_SIZE, d), page_table).reshape(nb, d)
            out["psk"].append(k1.reshape(batch, seq, N_HEADS, D_HEAD)); out["psv"].append(v1.reshape(batch, seq, N_HEADS, D_HEAD))
            out["ssk"].append(k2.reshape(nb, 1, N_HEADS, D_HEAD)); out["ssv"].append(v2.reshape(nb, 1, N_HEADS, D_HEAD))
        g_ffn = norm_ffn[i].reshape(1, d)
        w_gu = ffn_w_gu[i].astype(BF16)
        w_down = ffn_w_down[i].astype(BF16)
        xp = _out_ffn(xp, ap, w_out, g_ffn, w_gu, w_down)
        xs = _out_ffn(xs, a_s, w_out, g_ffn, w_gu, w_down)
    g_fin = norm_final.reshape(1, d)
    y_prompt = _final_norm(xp, g_fin).reshape(batch, seq, d)
    y_sample = _final_norm(xs, g_fin).reshape(nb, 1, d)
    st = {name: jnp.stack(vals) for name, vals in out.items()}
    return (y_prompt, y_sample, st["pmc"], st["pmn"], st["pmm"], st["smc"], st["smn"], st["smm"],
            st["pfk"], st["pfv"], st["pfl"], st["sfk"], st["sfv"], st["sfl"],
            st["psk"], st["psv"], st["ssk"], st["ssv"])
```

```python
import functools

import jax
import jax.numpy as jnp
from jax import lax
from jax.experimental import pallas as pl
from jax.experimental.pallas import tpu as pltpu

F32 = jnp.float32
BF16 = jnp.bfloat16

D_MODEL = 1024
DEPTH = 4
N_MIXERS = 3
NH_A = 8
DV_A = D_MODEL // NH_A
DQK_A = DV_A // 2
NQK_A = NH_A * DQK_A
NV_A = NH_A * DV_A
MLSTM_CHUNK = 64
GATE_CAP = 15.0
N_HEADS = 16
D_HEAD = D_MODEL // N_HEADS
PAGE_SIZE = 128
D_FF = ((8 * D_MODEL // 3 + 255) // 256) * 256
EPS = 1e-6
LOG2E = 1.4426950408889634

LANES = 128
MIB = 1024 * 1024

ROW_TILE = 512
FFN_ROW_TILE = 1024
FFN_COL_TILE = 256
ATT_TILE = 512
SB_SUB = 128
PAGES_PER_STEP = 8
HEADS_PER_PAIR = LANES // D_HEAD
BIAS_LANES = LANES // N_HEADS

NT_DIMS = (((1,), (1,)), ((), ()))
TN_DIMS = (((0,), (0,)), ((), ()))


def _params(semantics, vmem_mib):
    return pltpu.CompilerParams(dimension_semantics=semantics, vmem_limit_bytes=vmem_mib * MIB)


def _dot(a, b):
    return jnp.dot(a, b, preferred_element_type=F32)


def _dot_nt(a, b):
    return lax.dot_general(a, b, NT_DIMS, preferred_element_type=F32)


def _dot_exact(a, b):
    return jnp.dot(a, b, precision=lax.Precision.HIGHEST, preferred_element_type=F32)


def _split3(x):
    hi = x.astype(BF16)
    r = x - hi.astype(F32)
    mid = r.astype(BF16)
    return hi, mid, (r - mid.astype(F32)).astype(BF16)


def _rms(x, g):
    return x * lax.rsqrt(jnp.mean(x * x, axis=-1, keepdims=True) + EPS) * g


def _softplus(x):
    return jnp.maximum(x, 0.0) + jnp.log1p(jnp.exp(-jnp.abs(x)))


def _log_sigmoid(x):
    return -_softplus(-x)


def _sigmoid(x):
    return 1.0 / (1.0 + jnp.exp(-x))


def _iota2(shape, dim):
    return lax.broadcasted_iota(jnp.int32, shape, dim)


def _proj_kernel(x_ref, g_ref, *refs, plan):
    n_w = len(plan)
    w_refs, o_refs = refs[:n_w], refs[n_w:]
    h = _rms(x_ref[...], g_ref[...]).astype(BF16)
    k = 0
    for w_ref, (transposed, scale, dts) in zip(w_refs, plan):
        y = _dot_nt(w_ref[...], h) if transposed else _dot(h, w_ref[...])
        if scale != 1.0:
            y = y * scale
        for dt in dts:
            if transposed:
                o_refs[k][0] = y.astype(dt)
            else:
                o_refs[k][...] = y.astype(dt)
            k += 1


def _proj(x, g, ws, plan, batch):
    m = x.shape[0]
    tm = min(m, ROW_TILE)
    tiles_per_seq = m // batch // tm if any(p[0] for p in plan) else 1
    in_specs = [pl.BlockSpec((tm, D_MODEL), lambda i: (i, 0)), pl.BlockSpec((1, D_MODEL), lambda i: (0, 0))]
    in_specs += [pl.BlockSpec(w.shape, lambda i: (0, 0)) for w in ws]
    out_shape, out_specs = [], []
    for w, (transposed, _, dts) in zip(ws, plan):
        for dt in dts:
            if transposed:
                n = w.shape[0]
                out_shape.append(jax.ShapeDtypeStruct((batch, n, m // batch), dt))
                out_specs.append(pl.BlockSpec((1, n, tm), lambda i: (i // tiles_per_seq, 0, i % tiles_per_seq)))
            else:
                n = w.shape[1]
                out_shape.append(jax.ShapeDtypeStruct((m, n), dt))
                out_specs.append(pl.BlockSpec((tm, n), lambda i: (i, 0)))
    return pl.pallas_call(
        functools.partial(_proj_kernel, plan=plan),
        grid=(m // tm,), in_specs=in_specs, out_specs=out_specs, out_shape=out_shape,
        compiler_params=_params(("parallel",), 56), name="proj",
    )(x, g, *ws)


def _out_ffn_kernel(x_ref, a_ref, wo_ref, g_ref, wg_ref, wu_ref, wd_ref, o_ref, h_sc):
    f = pl.program_id(1)

    @pl.when(f == 0)
    def _():
        xm = x_ref[...] + _dot(a_ref[...].astype(BF16), wo_ref[...])
        o_ref[...] = xm
        h_sc[...] = _rms(xm, g_ref[...]).astype(BF16)

    h = h_sc[...]
    gate = _dot(h, wg_ref[...])
    up = _dot(h, wu_ref[...])
    act = (gate * _sigmoid(gate) * up).astype(BF16)
    o_ref[...] += _dot(act, wd_ref[...])


def _out_ffn(x, a, wo, g, wgu, wd):
    m = x.shape[0]
    tm = min(m, FFN_ROW_TILE)
    nf = D_FF // FFN_COL_TILE
    return pl.pallas_call(
        _out_ffn_kernel,
        grid=(m // tm, nf),
        in_specs=[
            pl.BlockSpec((tm, D_MODEL), lambda i, f: (i, 0)),
            pl.BlockSpec((tm, D_MODEL), lambda i, f: (i, 0)),
            pl.BlockSpec((D_MODEL, D_MODEL), lambda i, f: (0, 0)),
            pl.BlockSpec((1, D_MODEL), lambda i, f: (0, 0)),
            pl.BlockSpec((D_MODEL, FFN_COL_TILE), lambda i, f: (0, f)),
            pl.BlockSpec((D_MODEL, FFN_COL_TILE), lambda i, f: (0, nf + f)),
            pl.BlockSpec((FFN_COL_TILE, D_MODEL), lambda i, f: (f, 0)),
        ],
        out_specs=pl.BlockSpec((tm, D_MODEL), lambda i, f: (i, 0)),
        out_shape=jax.ShapeDtypeStruct((m, D_MODEL), F32),
        scratch_shapes=[pltpu.VMEM((tm, D_MODEL), BF16)],
        compiler_params=_params(("parallel", "arbitrary"), 48), name="out_ffn",
    )(x, a, wo, g, wgu, wgu, wd)


def _final_norm_kernel(x_ref, g_ref, o_ref):
    o_ref[...] = _rms(x_ref[...], g_ref[...])


def _final_norm(x, g):
    m = x.shape[0]
    tm = min(m, FFN_ROW_TILE)
    return pl.pallas_call(
        _final_norm_kernel, grid=(m // tm,),
        in_specs=[pl.BlockSpec((tm, D_MODEL), lambda i: (i, 0)), pl.BlockSpec((1, D_MODEL), lambda i: (0, 0))],
        out_specs=pl.BlockSpec((tm, D_MODEL), lambda i: (i, 0)),
        out_shape=jax.ShapeDtypeStruct((m, D_MODEL), F32),
        compiler_params=_params(("parallel",), 32), name="final_norm",
    )(x, g)


def _mlstm_gates(pre, bias):
    gates = GATE_CAP * jnp.tanh((pre + bias) / GATE_CAP)
    return gates[:, :NH_A], _log_sigmoid(gates[:, NH_A:])


def _mlstm_head_out(hc, gn, o):
    hn = hc * lax.rsqrt(jnp.mean(hc * hc, axis=-1, keepdims=True) + EPS)
    return hn * gn * _sigmoid(o)


def _mlstm_prompt_kernel(q_ref, k_ref, v_ref, o_ref, gt_ref, bg_ref, gn_ref, h_ref, c_ref, n_ref, m_ref):
    L = MLSTM_CHUNK

    @pl.when(pl.program_id(1) == 0)
    def _():
        c_ref[...] = jnp.zeros_like(c_ref)
        n_ref[...] = jnp.zeros_like(n_ref)
        m_ref[...] = jnp.zeros_like(m_ref)

    ig, lf = _mlstm_gates(gt_ref[...], bg_ref[...])
    row, col = _iota2((L, L), 0), _iota2((L, L), 1)
    causal = row >= col
    bcum = _dot_exact(causal.astype(F32), lf)
    r_t = (ig - bcum).T
    scale = DQK_A ** -0.5
    for h in range(NH_A):
        qs, vs = slice(h * DQK_A, (h + 1) * DQK_A), slice(h * DV_A, (h + 1) * DV_A)
        qh, kh, vh = q_ref[:, qs], k_ref[:, qs], v_ref[:, vs]
        c_old = c_ref[0, h]
        n_old = n_ref[0, h:h + 1, :]
        m_old = m_ref[0, :, h:h + 1]
        b_col = bcum[:, h:h + 1]
        a_col = b_col + m_old
        dm = jnp.where(causal, b_col + r_t[h:h + 1, :], -jnp.inf)
        mt = jnp.maximum(a_col, jnp.max(dm, axis=1, keepdims=True))
        w_int = jnp.exp(a_col - mt)
        w = jnp.exp(dm - mt) * (_dot_nt(qh, kh) * scale)
        num = w_int * _dot(qh, c_old.astype(BF16)) + _dot(w.astype(BF16), vh)
        den = w_int * jnp.sum(qh.astype(F32) * n_old, axis=1, keepdims=True) + jnp.sum(w, axis=1, keepdims=True)
        hc = num / jnp.maximum(jnp.abs(den), jnp.exp(-mt))
        h_ref[:, vs] = _mlstm_head_out(hc, gn_ref[:, vs], o_ref[:, vs]).astype(h_ref.dtype)
        b_last = b_col[L - 1:L, :]
        g_col = b_last - b_col + ig[:, h:h + 1]
        m_new = jnp.maximum(b_last + m_old, jnp.max(g_col, axis=0, keepdims=True))
        decay = jnp.exp(b_last + m_old - m_new)
        kw = kh.astype(F32) * (jnp.exp(g_col - m_new) * scale)
        c_ref[0, h] = decay * c_old + lax.dot_general(kw.astype(BF16), vh, TN_DIMS, preferred_element_type=F32)
        n_ref[0, h:h + 1, :] = decay * n_old + jnp.sum(kw, axis=0, keepdims=True)
        m_ref[0, :, h:h + 1] = m_new


def _mlstm_prompt(q, k, v, o, gates, b_gates, g_norm, batch):
    m = q.shape[0]
    nc = m // batch // MLSTM_CHUNK
    L = MLSTM_CHUNK
    rows = lambda b, c: (b * nc + c, 0)
    fixed = lambda b, c: (0, 0)
    return pl.pallas_call(
        _mlstm_prompt_kernel,
        grid=(batch, nc),
        in_specs=[
            pl.BlockSpec((L, NQK_A), rows), pl.BlockSpec((L, NQK_A), rows), pl.BlockSpec((L, NV_A), rows),
            pl.BlockSpec((L, NV_A), rows), pl.BlockSpec((L, 2 * NH_A), rows),
            pl.BlockSpec((1, 2 * NH_A), fixed), pl.BlockSpec((1, NV_A), fixed),
        ],
        out_specs=[
            pl.BlockSpec((L, NV_A), rows),
            pl.BlockSpec((1, NH_A, DQK_A, DV_A), lambda b, c: (b, 0, 0, 0)),
            pl.BlockSpec((1, NH_A, DQK_A), lambda b, c: (b, 0, 0)),
            pl.BlockSpec((1, 1, NH_A), lambda b, c: (b, 0, 0)),
        ],
        out_shape=[
            jax.ShapeDtypeStruct((m, NV_A), BF16),
            jax.ShapeDtypeStruct((batch, NH_A, DQK_A, DV_A), F32),
            jax.ShapeDtypeStruct((batch, NH_A, DQK_A), F32),
            jax.ShapeDtypeStruct((batch, 1, NH_A), F32),
        ],
        compiler_params=_params(("parallel", "arbitrary"), 32), name="mlstm_prompt",
    )(q, k, v, o, gates, b_gates, g_norm)


MLSTM_DEC_TILE = 8


def _mlstm_decode_kernel(q_ref, k_ref, v_ref, o_ref, gt_ref, bg_ref, gn_ref, c0_ref, n0_ref, m0_ref,
                         h_ref, c_ref, n_ref, m_ref):
    ig, lf = _mlstm_gates(gt_ref[...], bg_ref[...])
    scale = DQK_A ** -0.5
    q_t, k_t = q_ref[...].T, k_ref[...].T
    for j in range(MLSTM_DEC_TILE):
        for h in range(NH_A):
            qs, vs = slice(h * DQK_A, (h + 1) * DQK_A), slice(h * DV_A, (h + 1) * DV_A)
            q_row, k_row = q_ref[j:j + 1, qs], k_ref[j:j + 1, qs] * scale
            q_col, k_col = q_t[qs, j:j + 1], k_t[qs, j:j + 1] * scale
            v_row = v_ref[j:j + 1, vs]
            c_old, n_old, m_old = c0_ref[j, h], n0_ref[j, h:h + 1, :], m0_ref[j:j + 1, h:h + 1]
            i_g, a = ig[j:j + 1, h:h + 1], lf[j:j + 1, h:h + 1] + m_old
            mt = jnp.maximum(a, i_g)
            w_int = jnp.exp(a - mt)
            w = jnp.exp(i_g - mt) * jnp.sum(q_row * k_row, axis=1, keepdims=True)
            num = w_int * jnp.sum(q_col * c_old, axis=0, keepdims=True) + w * v_row
            den = w_int * jnp.sum(q_row * n_old, axis=1, keepdims=True) + w
            hc = num / jnp.maximum(jnp.abs(den), jnp.exp(-mt))
            h_ref[j:j + 1, vs] = _mlstm_head_out(hc, gn_ref[:, vs], o_ref[j:j + 1, vs])
            m_new = jnp.maximum(a, i_g)
            decay = jnp.exp(a - m_new)
            ws = jnp.exp(i_g - m_new)
            c_ref[j, h] = decay * c_old + (ws * k_col) * v_row
            n_ref[j, h:h + 1, :] = decay * n_old + ws * k_row
            m_ref[j:j + 1, h:h + 1] = m_new


def _mlstm_decode(q, k, v, o, gates, b_gates, g_norm, c0, n0, m0):
    nb = q.shape[0]
    tb = MLSTM_DEC_TILE
    rows = lambda i: (i, 0)
    fixed = lambda i: (0, 0)
    return pl.pallas_call(
        _mlstm_decode_kernel,
        grid=(nb // tb,),
        in_specs=[
            pl.BlockSpec((tb, NQK_A), rows), pl.BlockSpec((tb, NQK_A), rows), pl.BlockSpec((tb, NV_A), rows),
            pl.BlockSpec((tb, NV_A), rows), pl.BlockSpec((tb, 2 * NH_A), rows),
            pl.BlockSpec((1, 2 * NH_A), fixed), pl.BlockSpec((1, NV_A), fixed),
            pl.BlockSpec((tb, NH_A, DQK_A, DV_A), lambda i: (i, 0, 0, 0)),
            pl.BlockSpec((tb, NH_A, DQK_A), lambda i: (i, 0, 0)),
            pl.BlockSpec((tb, NH_A), rows),
        ],
        out_specs=[
            pl.BlockSpec((tb, NV_A), rows),
            pl.BlockSpec((tb, NH_A, DQK_A, DV_A), lambda i: (i, 0, 0, 0)),
            pl.BlockSpec((tb, NH_A, DQK_A), lambda i: (i, 0, 0)),
            pl.BlockSpec((tb, NH_A), rows),
        ],
        out_shape=[
            jax.ShapeDtypeStruct((nb, NV_A), F32),
            jax.ShapeDtypeStruct(c0.shape, F32),
            jax.ShapeDtypeStruct(n0.shape, F32),
            jax.ShapeDtypeStruct(m0.shape, F32),
        ],
        compiler_params=_params(("parallel",), 32), name="mlstm_decode",
    )(q, k, v, o, gates, b_gates, g_norm, c0, n0, m0)


def _fox_gates_kernel(fg_ref, bf_ref, lft_ref, qb_ref, kb_ref, d_sc):
    t = fg_ref.shape[0]
    lf = _log_sigmoid(fg_ref[...] + bf_ref[...])
    lft_ref[0] = lf.T
    tril = (_iota2((LANES, LANES), 0) >= _iota2((LANES, LANES), 1)).astype(F32)
    carry = jnp.zeros((1, N_HEADS), F32)
    for blk in range(t // LANES):
        rows = slice(blk * LANES, (blk + 1) * LANES)
        d = _dot_exact(tril, lf[rows]) + carry
        d_sc[rows, :] = d
        carry = d[LANES - 1:LANES, :]
    parts = _split3(d_sc[...] * LOG2E)
    head, lane = _iota2((N_HEADS, LANES), 0), _iota2((N_HEADS, LANES), 1)
    lane1 = _iota2((1, LANES), 1) % BIAS_LANES
    qb = jnp.where((lane1 >= 3) & (lane1 < 6), 1.0, 0.0)
    kb = jnp.where(lane1 < 3, 1.0, 0.0)
    for c, part in enumerate(parts):
        qb = qb + _dot(part, jnp.where(lane == head * BIAS_LANES + c, 1.0, 0.0).astype(BF16))
        kb = kb + _dot(part, jnp.where(lane == head * BIAS_LANES + 3 + c, -1.0, 0.0).astype(BF16))
    qb_ref[...] = qb.astype(BF16)
    kb_ref[...] = kb.astype(BF16)


def _fox_gates(fg, b_f, batch):
    m = fg.shape[0]
    t = m // batch
    return pl.pallas_call(
        _fox_gates_kernel,
        grid=(batch,),
        in_specs=[pl.BlockSpec((t, N_HEADS), lambda b: (b, 0)), pl.BlockSpec((1, N_HEADS), lambda b: (0, 0))],
        out_specs=[pl.BlockSpec((1, N_HEADS, t), lambda b: (b, 0, 0)),
                   pl.BlockSpec((t, LANES), lambda b: (b, 0)), pl.BlockSpec((t, LANES), lambda b: (b, 0))],
        out_shape=[jax.ShapeDtypeStruct((batch, N_HEADS, t), F32),
                   jax.ShapeDtypeStruct((m, LANES), BF16), jax.ShapeDtypeStruct((m, LANES), BF16)],
        scratch_shapes=[pltpu.VMEM((t, N_HEADS), F32)],
        compiler_params=_params(("parallel",), 32), name="fox_gates",
    )(fg, b_f)


def _head_lane_mask(hh):
    lane = _iota2((1, LANES), 1)
    return jnp.where((lane >= hh * D_HEAD) & (lane < (hh + 1) * D_HEAD), 1.0, 0.0).astype(BF16)


def _fox_prompt_kernel(q_ref, k_ref, vt_ref, qb_ref, kb_ref, o_ref, m_sc, acc_sc):
    qi, ki = pl.program_id(1), pl.program_id(2)
    tq, tk = q_ref.shape[0], k_ref.shape[0]
    n_pairs = N_HEADS // HEADS_PER_PAIR

    @pl.when(ki == 0)
    def _():
        m_sc[...] = jnp.full_like(m_sc, -jnp.inf)
        acc_sc[...] = jnp.zeros_like(acc_sc)

    def block(diagonal):
        qb, kb = qb_ref[...], kb_ref[...]
        ones = jnp.ones((D_HEAD, tk), BF16)
        bias_head = _iota2((1, LANES), 1) // BIAS_LANES
        visible = _iota2((tk, tq), 0) <= _iota2((tk, tq), 1)

        def pair(hp, carry):
            off = pl.multiple_of(hp * LANES, LANES)
            rhs = jnp.concatenate([q_ref[:, pl.ds(off, LANES)], qb], axis=1)
            kp = k_ref[:, pl.ds(off, LANES)]
            vtp = vt_ref[0, pl.ds(off, LANES), :]
            for hh in range(HEADS_PER_PAIR):
                head = hp * HEADS_PER_PAIR + hh
                kb_h = kb * jnp.where(bias_head == head, 1.0, 0.0).astype(BF16)
                s = _dot_nt(jnp.concatenate([kp * _head_lane_mask(hh), kb_h], axis=1), rhs)
                if diagonal:
                    s = jnp.where(visible, s, -jnp.inf)
                m_old = m_sc[head]
                m_new = jnp.maximum(m_old, jnp.max(s, axis=0, keepdims=True))
                p = jnp.exp2(s - m_new).astype(BF16)
                v_aug = (jnp.concatenate([vtp[:D_HEAD], ones], axis=0) if hh == 0
                         else jnp.concatenate([ones, vtp[D_HEAD:]], axis=0))
                acc_sc[head] = jnp.exp2(m_old - m_new) * acc_sc[head] + _dot(v_aug, p)
                m_sc[head] = m_new
            return carry

        lax.fori_loop(0, n_pairs, pair, 0)

    @pl.when(ki < qi)
    def _():
        block(False)

    @pl.when(ki == qi)
    def _():
        block(True)

    @pl.when(ki == pl.num_programs(2) - 1)
    def _():
        def pair(hp, carry):
            off = pl.multiple_of(hp * LANES, LANES)
            a0, a1 = acc_sc[hp * HEADS_PER_PAIR], acc_sc[hp * HEADS_PER_PAIR + 1]
            o_t = jnp.concatenate([a0[:D_HEAD] * (1.0 / a0[D_HEAD:D_HEAD + 1]),
                                   a1[D_HEAD:] * (1.0 / a1[:1])], axis=0)
            o_ref[:, pl.ds(off, LANES)] = o_t.T.astype(o_ref.dtype)
            return carry

        lax.fori_loop(0, n_pairs, pair, 0)


def _fox_prompt(q, k, vt, qb, kb, batch):
    m = q.shape[0]
    t = m // batch
    tile = ATT_TILE
    n = t // tile
    qmap = lambda b, qi, ki: (b * n + qi, 0)
    kmap = lambda b, qi, ki: (b * n + jnp.minimum(ki, qi), 0)
    return pl.pallas_call(
        _fox_prompt_kernel,
        grid=(batch, n, n),
        in_specs=[
            pl.BlockSpec((tile, D_MODEL), qmap), pl.BlockSpec((tile, D_MODEL), kmap),
            pl.BlockSpec((1, D_MODEL, tile), lambda b, qi, ki: (b, 0, jnp.minimum(ki, qi))),
            pl.BlockSpec((tile, LANES), qmap), pl.BlockSpec((tile, LANES), kmap),
        ],
        out_specs=pl.BlockSpec((tile, D_MODEL), qmap),
        out_shape=jax.ShapeDtypeStruct((m, D_MODEL), BF16),
        scratch_shapes=[pltpu.VMEM((N_HEADS, 1, tile), F32), pltpu.VMEM((N_HEADS, LANES, tile), F32)],
        compiler_params=_params(("parallel", "parallel", "arbitrary"), 40), name="fox_prompt",
    )(q, k, vt, qb, kb)


SUFFIX_ROWS = SB_SUB + 16


def _suffix_lhs():
    r, c = _iota2((SUFFIX_ROWS, SB_SUB), 0), _iota2((SUFFIX_ROWS, SB_SUB), 1)
    u = jnp.where((c > r) | (r >= SB_SUB), 1.0, 0.0).astype(BF16)
    return jnp.concatenate([u, u], axis=1)


def _sb_prompt_kernel(q_ref, k_ref, vt_ref, o_ref, carry_sc, acc_sc):
    qi, kk = pl.program_id(1), pl.program_id(2)
    tq, tk = q_ref.shape[0], k_ref.shape[0]
    n_pairs = N_HEADS // HEADS_PER_PAIR
    n_sub = tk // SB_SUB

    @pl.when(kk == 0)
    def _():
        carry_sc[...] = jnp.zeros_like(carry_sc)
        acc_sc[...] = jnp.zeros_like(acc_sc)

    def block(diagonal):
        suffix_lhs = _suffix_lhs()
        zeros = jnp.zeros((D_HEAD, tk), BF16)
        visible = _iota2((tk, tq), 0) < _iota2((tk, tq), 1)

        def pair(hp, carry):
            off = pl.multiple_of(hp * LANES, LANES)
            qp = q_ref[:, pl.ds(off, LANES)]
            kp = k_ref[:, pl.ds(off, LANES)]
            vtp = vt_ref[0, pl.ds(off, LANES), :]
            for hh in range(HEADS_PER_PAIR):
                head = hp * HEADS_PER_PAIR + hh
                z = _dot_nt(kp * _head_lane_mask(hh), qp)
                nl = jnp.maximum(z, 0.0) + jnp.log2(1.0 + jnp.exp2(-jnp.abs(z)))
                if diagonal:
                    nl = jnp.where(visible, nl, 0.0)
                hi = nl.astype(BF16)
                lo = (nl - hi.astype(F32)).astype(BF16)
                newer = carry_sc[head]
                later = [None] * n_sub
                for sub in reversed(range(n_sub)):
                    keys = slice(sub * SB_SUB, (sub + 1) * SB_SUB)
                    sums = _dot(suffix_lhs, jnp.concatenate([hi[keys], lo[keys]], axis=0))
                    later[sub] = sums[:SB_SUB] + newer
                    newer = newer + sums[SB_SUB:SB_SUB + 1]
                carry_sc[head] = newer
                a = jnp.exp2(z - nl - jnp.concatenate(later, axis=0))
                if diagonal:
                    a = jnp.where(visible, a, 0.0)
                v_h = (jnp.concatenate([vtp[:D_HEAD], zeros], axis=0) if hh == 0
                       else jnp.concatenate([zeros, vtp[D_HEAD:]], axis=0))
                acc_sc[hp] += _dot(v_h, a.astype(BF16))
            return carry

        lax.fori_loop(0, n_pairs, pair, 0)

    @pl.when(kk == 0)
    def _():
        block(True)

    @pl.when((kk > 0) & (kk <= qi))
    def _():
        block(False)

    @pl.when(kk == pl.num_programs(2) - 1)
    def _():
        def pair(hp, carry):
            off = pl.multiple_of(hp * LANES, LANES)
            o_ref[:, pl.ds(off, LANES)] = acc_sc[hp].T.astype(o_ref.dtype)
            return carry

        lax.fori_loop(0, n_pairs, pair, 0)


def _sb_prompt(q, k, vt, batch):
    m = q.shape[0]
    t = m // batch
    tile = ATT_TILE
    n = t // tile
    qmap = lambda b, qi, kk: (b * n + qi, 0)
    return pl.pallas_call(
        _sb_prompt_kernel,
        grid=(batch, n, n),
        in_specs=[
            pl.BlockSpec((tile, D_MODEL), qmap),
            pl.BlockSpec((tile, D_MODEL), lambda b, qi, kk: (b * n + jnp.maximum(qi - kk, 0), 0)),
            pl.BlockSpec((1, D_MODEL, tile), lambda b, qi, kk: (b, 0, jnp.maximum(qi - kk, 0))),
        ],
        out_specs=pl.BlockSpec((tile, D_MODEL), qmap),
        out_shape=jax.ShapeDtypeStruct((m, D_MODEL), BF16),
        scratch_shapes=[pltpu.VMEM((N_HEADS, 1, tile), F32),
                        pltpu.VMEM((N_HEADS // HEADS_PER_PAIR, LANES, tile), F32)],
        compiler_params=_params(("parallel", "parallel", "arbitrary"), 40), name="sb_prompt",
    )(q, k, vt)


def _page_spec(block, n_pages, slot):
    p = PAGES_PER_STEP
    return pl.BlockSpec(block, lambda b, j, pt: (pt[b, n_pages - 1 - (j * p + slot)],) + (0,) * (len(block) - 1))


def _load_query_columns(q_row, qcol_sc):
    for h in range(N_HEADS):
        qcol_sc[h] = jnp.broadcast_to(q_row[:, h * D_HEAD:(h + 1) * D_HEAD].T, (D_HEAD, PAGE_SIZE))


def _page_scores(k_refs, qcol_sc):
    q = qcol_sc[...]
    return jnp.concatenate([jnp.sum(k_ref[0] * q, axis=1) for k_ref in k_refs], axis=1)


def _page_suffix_sums(x, carry):
    p = x.shape[1] // PAGE_SIZE
    r, c = _iota2((PAGE_SIZE, 2 * PAGE_SIZE), 0), _iota2((PAGE_SIZE, 2 * PAGE_SIZE), 1)
    u = jnp.where((r > c) | (c >= PAGE_SIZE), 1.0, 0.0).astype(BF16)
    stacked = jnp.concatenate([x[:, i * PAGE_SIZE:(i + 1) * PAGE_SIZE] for i in range(p)], axis=0)
    sums = sum(_dot(part, u) for part in _split3(stacked))
    out = []
    for i in range(p):
        rows = slice(i * N_HEADS, (i + 1) * N_HEADS)
        out.append(sums[rows, :PAGE_SIZE] + carry)
        carry = carry + sums[rows, PAGE_SIZE:]
    return jnp.concatenate(out, axis=1), carry


def _accumulate_values(v_refs, w_sc, acc_sc, rescale):
    for h in range(N_HEADS):
        acc = acc_sc[h]
        if rescale is not None:
            acc = acc * rescale[h:h + 1, :]
        for i, v_ref in enumerate(v_refs):
            acc = acc + w_sc[h:h + 1, i * PAGE_SIZE:(i + 1) * PAGE_SIZE] * v_ref[0, h]
        acc_sc[h] = acc


def _store_head_sums(acc_sc, inv, o_ref):
    ones = jnp.ones((8, PAGE_SIZE), BF16)
    for h in range(N_HEADS):
        row = sum(_dot_nt(ones, part) for part in _split3(acc_sc[h]))[:1]
        if inv is not None:
            row = row * inv[h:h + 1, :]
        o_ref[0, :, h * D_HEAD:(h + 1) * D_HEAD] = row


def _fox_decode_kernel(pt_ref, q_ref, kn_ref, vn_ref, lfn_ref, *refs):
    p = PAGES_PER_STEP
    k_refs, v_refs, lf_refs = refs[:p], refs[p:2 * p], refs[2 * p:3 * p]
    o_ref, qcol_sc, w_sc, m_sc, l_sc, carry_sc, acc_sc = refs[3 * p:]
    j = pl.program_id(1)

    @pl.when(j == 0)
    def _():
        q = q_ref[0] * (D_HEAD ** -0.5)
        _load_query_columns(q, qcol_sc)
        seg = (_iota2((D_MODEL, N_HEADS), 0) // D_HEAD == _iota2((D_MODEL, N_HEADS), 1)).astype(F32)
        m_sc[...] = _dot_exact(q * kn_ref[0], seg).T
        l_sc[...] = jnp.ones_like(l_sc)
        lane0 = _iota2((D_HEAD, PAGE_SIZE), 1) == 0
        for h in range(N_HEADS):
            acc_sc[h] = jnp.where(lane0, vn_ref[0][:, h * D_HEAD:(h + 1) * D_HEAD].T, 0.0)
        carry_sc[...] = jnp.broadcast_to(lfn_ref[0].T, carry_sc.shape)

    logf = jnp.concatenate([lf_ref[0] for lf_ref in lf_refs], axis=1)
    bias, carry = _page_suffix_sums(logf, carry_sc[...])
    carry_sc[...] = carry
    s = _page_scores(k_refs, qcol_sc) + bias
    m_old = m_sc[...]
    m_new = jnp.maximum(m_old, jnp.max(s, axis=1, keepdims=True))
    alpha = jnp.exp(m_old - m_new)
    w = jnp.exp(s - m_new)
    w_sc[...] = w
    l_sc[...] = alpha * l_sc[...] + jnp.sum(w, axis=1, keepdims=True)
    m_sc[...] = m_new
    _accumulate_values(v_refs, w_sc, acc_sc, alpha)

    @pl.when(j == pl.num_programs(1) - 1)
    def _():
        _store_head_sums(acc_sc, 1.0 / l_sc[...], o_ref)


def _decode_scratch():
    p = PAGES_PER_STEP
    return [pltpu.VMEM((N_HEADS, D_HEAD, PAGE_SIZE), F32), pltpu.VMEM((N_HEADS, p * PAGE_SIZE), F32)]


def _fox_decode(q, k_new, v_new, lf_new, cache_k, cache_v, cache_lf, page_table):
    nb, n_pages = page_table.shape
    p = PAGES_PER_STEP
    row = lambda b, j, pt: (b, 0, 0)
    in_specs = [pl.BlockSpec((1, 1, D_MODEL), row)] * 3 + [pl.BlockSpec((1, 1, N_HEADS), row)]
    in_specs += [_page_spec((1, N_HEADS, D_HEAD, PAGE_SIZE), n_pages, i) for i in range(p)] * 2
    in_specs += [_page_spec((1, N_HEADS, PAGE_SIZE), n_pages, i) for i in range(p)]
    grid_spec = pltpu.PrefetchScalarGridSpec(
        num_scalar_prefetch=1, grid=(nb, n_pages // p), in_specs=in_specs,
        out_specs=pl.BlockSpec((1, 1, D_MODEL), row),
        scratch_shapes=_decode_scratch() + [
            pltpu.VMEM((N_HEADS, 1), F32), pltpu.VMEM((N_HEADS, 1), F32), pltpu.VMEM((N_HEADS, PAGE_SIZE), F32),
            pltpu.VMEM((N_HEADS, D_HEAD, PAGE_SIZE), F32)])
    return pl.pallas_call(
        _fox_decode_kernel, grid_spec=grid_spec,
        out_shape=jax.ShapeDtypeStruct((nb, 1, D_MODEL), F32),
        compiler_params=_params(("parallel", "arbitrary"), 40), name="fox_decode",
    )(page_table, q, k_new, v_new, lf_new, *([cache_k] * p), *([cache_v] * p), *([cache_lf] * p))


def _sb_decode_kernel(pt_ref, q_ref, *refs):
    p = PAGES_PER_STEP
    k_refs, v_refs = refs[:p], refs[p:2 * p]
    o_ref, qcol_sc, w_sc, carry_sc, acc_sc = refs[2 * p:]
    j = pl.program_id(1)

    @pl.when(j == 0)
    def _():
        _load_query_columns(q_ref[0] * (D_HEAD ** -0.5), qcol_sc)
        carry_sc[...] = jnp.zeros_like(carry_sc)
        acc_sc[...] = jnp.zeros_like(acc_sc)

    z = _page_scores(k_refs, qcol_sc)
    nl = _softplus(z)
    later, carry = _page_suffix_sums(nl, carry_sc[...])
    carry_sc[...] = carry
    w_sc[...] = jnp.exp(z - nl - later)
    _accumulate_values(v_refs, w_sc, acc_sc, None)

    @pl.when(j == pl.num_programs(1) - 1)
    def _():
        _store_head_sums(acc_sc, None, o_ref)


def _sb_decode(q, cache_k, cache_v, page_table):
    nb, n_pages = page_table.shape
    p = PAGES_PER_STEP
    row = lambda b, j, pt: (b, 0, 0)
    in_specs = [pl.BlockSpec((1, 1, D_MODEL), row)]
    in_specs += [_page_spec((1, N_HEADS, D_HEAD, PAGE_SIZE), n_pages, i) for i in range(p)] * 2
    grid_spec = pltpu.PrefetchScalarGridSpec(
        num_scalar_prefetch=1, grid=(nb, n_pages // p), in_specs=in_specs,
        out_specs=pl.BlockSpec((1, 1, D_MODEL), row),
        scratch_shapes=_decode_scratch() + [
            pltpu.VMEM((N_HEADS, PAGE_SIZE), F32), pltpu.VMEM((N_HEADS, D_HEAD, PAGE_SIZE), F32)])
    return pl.pallas_call(
        _sb_decode_kernel, grid_spec=grid_spec,
        out_shape=jax.ShapeDtypeStruct((nb, 1, D_MODEL), F32),
        compiler_params=_params(("parallel", "arbitrary"), 40), name="sb_decode",
    )(page_table, q, *([cache_k] * p), *([cache_v] * p))


def _fox_logf_kernel(fg_ref, bf_ref, lf_ref):
    lf_ref[...] = _log_sigmoid(fg_ref[...] + bf_ref[...])


def _fox_logf(fg, b_f):
    return pl.pallas_call(_fox_logf_kernel, out_shape=jax.ShapeDtypeStruct(fg.shape, F32), name="fox_logf")(fg, b_f)


def _heads_last(x_t, batch, seq):
    return x_t.reshape(batch, N_HEADS, D_HEAD, seq).transpose(0, 3, 1, 2)


def _pages_token_minor(cache):
    return cache.transpose(0, 2, 3, 1)


def kernel(x_prompt, x_sample, state_mlstm_c, state_mlstm_n, state_mlstm_m, cache_fox_k, cache_fox_v, cache_fox_logf, cache_sb_k, cache_sb_v, page_table, norm_mix, norm_ffn, norm_final, mlstm_w_in, mlstm_b_gates, mlstm_norm, mlstm_w_out, fox_w_in, fox_b_f, fox_w_out, sb_w_in, sb_w_out, ffn_w_gu, ffn_w_down):
    batch, seq, d = x_prompt.shape
    nb = x_sample.shape[0]
    xp = x_prompt.reshape(batch * seq, d)
    xs = x_sample.reshape(nb, d)
    q_scale = D_HEAD ** -0.5 * LOG2E
    out = {name: [] for name in ("pmc", "pmn", "pmm", "smc", "smn", "smm", "pfk", "pfv", "pfl", "sfk", "sfv", "sfl",
                                 "psk", "psv", "ssk", "ssv")}
    for i in range(DEPTH):
        j = i // N_MIXERS
        g_mix = norm_mix[i].reshape(1, d)
        if i % N_MIXERS == 0:
            w_in = mlstm_w_in[j].astype(BF16)
            splits = (NQK_A, 2 * NQK_A, 2 * NQK_A + NV_A, 2 * NQK_A + 2 * NV_A)
            ws = [w_in[:, a:b] for a, b in zip((0,) + splits, splits + (w_in.shape[1],))]
            b_gates = mlstm_b_gates[j].reshape(1, 2 * NH_A)
            g_norm = mlstm_norm[j].reshape(1, NV_A)
            w_out = mlstm_w_out[j].astype(BF16)
            plan = tuple((False, 1.0, (dt,)) for dt in (BF16, BF16, BF16, F32, F32))
            q, k, v, o, gt = _proj(xp, g_mix, ws, plan, batch)
            ap, c1, n1, m1 = _mlstm_prompt(q, k, v, o, gt, b_gates, g_norm, batch)
            q, k, v, o, gt = _proj(xs, g_mix, ws, ((False, 1.0, (F32,)),) * 5, nb)
            a_s, c2, n2, m2 = _mlstm_decode(q, k, v, o, gt, b_gates, g_norm,
                                            state_mlstm_c[j], state_mlstm_n[j], state_mlstm_m[j])
            out["pmc"].append(c1); out["pmn"].append(n1); out["pmm"].append(m1.reshape(batch, NH_A))
            out["smc"].append(c2); out["smn"].append(n2); out["smm"].append(m2)
        elif i % N_MIXERS == 1:
            w_in = fox_w_in[j].astype(BF16)
            wq, wk, wv, wf = (w_in[:, a:b] for a, b in ((0, d), (d, 2 * d), (2 * d, 3 * d), (3 * d, 3 * d + N_HEADS)))
            b_f = fox_b_f[j].reshape(1, N_HEADS)
            w_out = fox_w_out[j].astype(BF16)
            plan = ((False, q_scale, (BF16,)), (False, 1.0, (BF16,)), (True, 1.0, (F32,)), (True, 1.0, (F32, BF16)),
                    (False, 1.0, (F32,)))
            q, kb16, k1t, v1t, vt16, fg = _proj(xp, g_mix, [wq, wk, wk.T, wv.T, wf], plan, batch)
            l1t, qbias, kbias = _fox_gates(fg, b_f, batch)
            ap = _fox_prompt(q, kb16, vt16, qbias, kbias, batch)
            q, k2, v2, fg = _proj(xs, g_mix, [wq, wk, wv, wf], ((False, 1.0, (F32,)),) * 4, nb)
            l2 = _fox_logf(fg, b_f)
            a_s = _fox_decode(q.reshape(nb, 1, d), k2.reshape(nb, 1, d), v2.reshape(nb, 1, d),
                              l2.reshape(nb, 1, N_HEADS),
                              _pages_token_minor(cache_fox_k[j]), _pages_token_minor(cache_fox_v[j]),
                              cache_fox_logf[j].transpose(0, 2, 1), page_table).reshape(nb, d)
            out["pfk"].append(_heads_last(k1t, batch, seq)); out["pfv"].append(_heads_last(v1t, batch, seq))
            out["pfl"].append(l1t.transpose(0, 2, 1))
            out["sfk"].append(k2.reshape(nb, 1, N_HEADS, D_HEAD)); out["sfv"].append(v2.reshape(nb, 1, N_HEADS, D_HEAD))
            out["sfl"].append(l2.reshape(nb, 1, N_HEADS))
        else:
            w_in = sb_w_in[j].astype(BF16)
            wq, wk, wv = (w_in[:, a:b] for a, b in ((0, d), (d, 2 * d), (2 * d, 3 * d)))
            w_out = sb_w_out[j].astype(BF16)
            plan = ((False, q_scale, (BF16,)), (False, 1.0, (BF16,)), (True, 1.0, (F32,)), (True, 1.0, (F32, BF16)))
            q, kb16, k1t, v1t, vt16 = _proj(xp, g_mix, [wq, wk, wk.T, wv.T], plan, batch)
            ap = _sb_prompt(q, kb16, vt16, batch)
            q, k2, v2 = _proj(xs, g_mix, [wq, wk, wv], ((False, 1.0, (F32,)),) * 3, nb)
            a_s = _sb_decode(q.reshape(nb, 1, d), _pages_token_minor(cache_sb_k[j]), _pages_token_minor(cache_sb_v[j]),
                             page_table).reshape(nb, d)
            out["psk"].append(_heads_last(k1t, batch, seq)); out["psv"].append(_heads_last(v1t, batch, seq))
            out["ssk"].append(k2.reshape(nb, 1, N_HEADS, D_HEAD)); out["ssv"].append(v2.reshape(nb, 1, N_HEADS, D_HEAD))
        g_ffn = norm_ffn[i].reshape(1, d)
        w_gu = ffn_w_gu[i].astype(BF16)
        w_down = ffn_w_down[i].astype(BF16)
        xp = _out_ffn(xp, ap, w_out, g_ffn, w_gu, w_down)
        xs = _out_ffn(xs, a_s, w_out, g_ffn, w_gu, w_down)
    g_fin = norm_final.reshape(1, d)
    y_prompt = _final_norm(xp, g_fin).reshape(batch, seq, d)
    y_sample = _final_norm(xs, g_fin).reshape(nb, 1, d)
    st = {name: jnp.stack(vals) for name, vals in out.items()}
    return (y_prompt, y_sample, st["pmc"], st["pmn"], st["pmm"], st["smc"], st["smn"], st["smm"],
            st["pfk"], st["pfv"], st["pfl"], st["sfk"], st["sfv"], st["sfl"],
            st["psk"], st["psv"], st["ssk"], st["ssv"])
```

```python
import functools

import jax
import jax.numpy as jnp
from jax import lax
from jax.experimental import pallas as pl
from jax.experimental.pallas import tpu as pltpu

F32 = jnp.float32
BF16 = jnp.bfloat16

D_MODEL = 1024
DEPTH = 4
N_MIXERS = 3
NH_A = 8
DV_A = D_MODEL // NH_A
DQK_A = DV_A // 2
NQK_A = NH_A * DQK_A
NV_A = NH_A * DV_A
MLSTM_CHUNK = 64
GATE_CAP = 15.0
N_HEADS = 16
D_HEAD = D_MODEL // N_HEADS
PAGE_SIZE = 128
D_FF = ((8 * D_MODEL // 3 + 255) // 256) * 256
EPS = 1e-6
LOG2E = 1.4426950408889634

LANES = 128
MIB = 1024 * 1024

ROW_TILE = 512
FFN_ROW_TILE = 1024
FFN_COL_TILE = 256
ATT_TILE = 512
SB_SUB = 128
PAGES_PER_STEP = 8
HEADS_PER_PAIR = LANES // D_HEAD
BIAS_LANES = LANES // N_HEADS

NT_DIMS = (((1,), (1,)), ((), ()))
TN_DIMS = (((0,), (0,)), ((), ()))


def _params(semantics, vmem_mib):
    return pltpu.CompilerParams(dimension_semantics=semantics, vmem_limit_bytes=vmem_mib * MIB)


def _dot(a, b):
    return jnp.dot(a, b, preferred_element_type=F32)


def _dot_nt(a, b):
    return lax.dot_general(a, b, NT_DIMS, preferred_element_type=F32)


def _dot_exact(a, b):
    return jnp.dot(a, b, precision=lax.Precision.HIGHEST, preferred_element_type=F32)


def _split3(x):
    hi = x.astype(BF16)
    r = x - hi.astype(F32)
    mid = r.astype(BF16)
    return hi, mid, (r - mid.astype(F32)).astype(BF16)


def _rms(x, g):
    return x * lax.rsqrt(jnp.mean(x * x, axis=-1, keepdims=True) + EPS) * g


def _softplus(x):
    return jnp.maximum(x, 0.0) + jnp.log1p(jnp.exp(-jnp.abs(x)))


def _log_sigmoid(x):
    return -_softplus(-x)


def _sigmoid(x):
    return 1.0 / (1.0 + jnp.exp(-x))


def _iota2(shape, dim):
    return lax.broadcasted_iota(jnp.int32, shape, dim)


def _proj_kernel(x_ref, g_ref, *refs, plan):
    n_w = len(plan)
    w_refs, o_refs = refs[:n_w], refs[n_w:]
    h = _rms(x_ref[...], g_ref[...]).astype(BF16)
    k = 0
    for w_ref, (transposed, scale, dts) in zip(w_refs, plan):
        y = _dot_nt(w_ref[...], h) if transposed else _dot(h, w_ref[...])
        if scale != 1.0:
            y = y * scale
        for dt in dts:
            if transposed:
                o_refs[k][0] = y.astype(dt)
            else:
                o_refs[k][...] = y.astype(dt)
            k += 1


def _proj(x, g, ws, plan, batch):
    m = x.shape[0]
    tm = min(m, ROW_TILE)
    tiles_per_seq = m // batch // tm if any(p[0] for p in plan) else 1
    in_specs = [pl.BlockSpec((tm, D_MODEL), lambda i: (i, 0)), pl.BlockSpec((1, D_MODEL), lambda i: (0, 0))]
    in_specs += [pl.BlockSpec(w.shape, lambda i: (0, 0)) for w in ws]
    out_shape, out_specs = [], []
    for w, (transposed, _, dts) in zip(ws, plan):
        for dt in dts:
            if transposed:
                n = w.shape[0]
                out_shape.append(jax.ShapeDtypeStruct((batch, n, m // batch), dt))
                out_specs.append(pl.BlockSpec((1, n, tm), lambda i: (i // tiles_per_seq, 0, i % tiles_per_seq)))
            else:
                n = w.shape[1]
                out_shape.append(jax.ShapeDtypeStruct((m, n), dt))
                out_specs.append(pl.BlockSpec((tm, n), lambda i: (i, 0)))
    return pl.pallas_call(
        functools.partial(_proj_kernel, plan=plan),
        grid=(m // tm,), in_specs=in_specs, out_specs=out_specs, out_shape=out_shape,
        compiler_params=_params(("parallel",), 56), name="proj",
    )(x, g, *ws)


def _out_ffn_kernel(x_ref, a_ref, wo_ref, g_ref, wg_ref, wu_ref, wd_ref, o_ref, h_sc):
    f = pl.program_id(1)

    @pl.when(f == 0)
    def _():
        xm = x_ref[...] + _dot(a_ref[...].astype(BF16), wo_ref[...])
        o_ref[...] = xm
        h_sc[...] = _rms(xm, g_ref[...]).astype(BF16)

    h = h_sc[...]
    gate = _dot(h, wg_ref[...])
    up = _dot(h, wu_ref[...])
    act = (gate * _sigmoid(gate) * up).astype(BF16)
    o_ref[...] += _dot(act, wd_ref[...])


def _out_ffn(x, a, wo, g, wgu, wd):
    m = x.shape[0]
    tm = min(m, FFN_ROW_TILE)
    nf = D_FF // FFN_COL_TILE
    return pl.pallas_call(
        _out_ffn_kernel,
        grid=(m // tm, nf),
        in_specs=[
            pl.BlockSpec((tm, D_MODEL), lambda i, f: (i, 0)),
            pl.BlockSpec((tm, D_MODEL), lambda i, f: (i, 0)),
            pl.BlockSpec((D_MODEL, D_MODEL), lambda i, f: (0, 0)),
            pl.BlockSpec((1, D_MODEL), lambda i, f: (0, 0)),
            pl.BlockSpec((D_MODEL, FFN_COL_TILE), lambda i, f: (0, f)),
            pl.BlockSpec((D_MODEL, FFN_COL_TILE), lambda i, f: (0, nf + f)),
            pl.BlockSpec((FFN_COL_TILE, D_MODEL), lambda i, f: (f, 0)),
        ],
        out_specs=pl.BlockSpec((tm, D_MODEL), lambda i, f: (i, 0)),
        out_shape=jax.ShapeDtypeStruct((m, D_MODEL), F32),
        scratch_shapes=[pltpu.VMEM((tm, D_MODEL), BF16)],
        compiler_params=_params(("parallel", "arbitrary"), 48), name="out_ffn",
    )(x, a, wo, g, wgu, wgu, wd)


def _final_norm_kernel(x_ref, g_ref, o_ref):
    o_ref[...] = _rms(x_ref[...], g_ref[...])


def _final_norm(x, g):
    m = x.shape[0]
    tm = min(m, FFN_ROW_TILE)
    return pl.pallas_call(
        _final_norm_kernel, grid=(m // tm,),
        in_specs=[pl.BlockSpec((tm, D_MODEL), lambda i: (i, 0)), pl.BlockSpec((1, D_MODEL), lambda i: (0, 0))],
        out_specs=pl.BlockSpec((tm, D_MODEL), lambda i: (i, 0)),
        out_shape=jax.ShapeDtypeStruct((m, D_MODEL), F32),
        compiler_params=_params(("parallel",), 32), name="final_norm",
    )(x, g)


MLSTM_TILE = LANES


def _mlstm_gates(pre, bias):
    gates = GATE_CAP * jnp.tanh((pre + bias) / GATE_CAP)
    return gates[:, :NH_A], _log_sigmoid(gates[:, NH_A:])


def _mlstm_head_out(hc, gn, o):
    hn = hc * lax.rsqrt(jnp.mean(hc * hc, axis=-1, keepdims=True) + EPS)
    return hn * gn * _sigmoid(o)


def _mlstm_prompt_kernel(qt_ref, k_ref, vt_ref, ot_ref, gt_ref, bg_ref, gn_ref, h_ref, c_ref, n_ref, m_ref,
                         state_sc, m_sc):
    L = MLSTM_TILE
    aug = DV_A + 16

    @pl.when(pl.program_id(1) == 0)
    def _():
        state_sc[...] = jnp.zeros_like(state_sc)
        m_sc[...] = jnp.zeros_like(m_sc)

    gates = GATE_CAP * jnp.tanh((gt_ref[0] + bg_ref[...]) / GATE_CAP)
    ig, lf = gates[:NH_A], _log_sigmoid(gates[NH_A:])
    r_i, c_i = _iota2((L, 2 * L), 0), _iota2((L, 2 * L), 1)
    prefix = jnp.where((r_i <= c_i) | (c_i >= L), 1.0, 0.0).astype(BF16)
    sums = sum(_dot(part, prefix) for part in _split3(lf))
    bcum, btot = sums[:, :L], sums[:, L:]
    r = ig - bcum
    ones8, zeros16 = jnp.ones((NH_A, L), F32), jnp.zeros((2 * NH_A, L), F32)
    r_parts, b_parts = _split3(r), _split3(bcum)
    lhs_t = jnp.concatenate([p.astype(F32) for p in r_parts] + [ones8] * 3 + [zeros16], axis=0).T.astype(BF16)
    rhs_all = jnp.concatenate([ones8] * 3 + [p.astype(F32) for p in b_parts] + [zeros16], axis=0)
    rhs_head = _iota2((8 * NH_A, 1), 0) % NH_A
    causal = _iota2((L, L), 0) <= _iota2((L, L), 1)
    ones_rows = jnp.ones((aug - DV_A, L), BF16)
    scale = DQK_A ** -0.5
    for h in range(NH_A):
        hp, hh = divmod(h, HEADS_PER_PAIR)
        pair, vs = slice(hp * LANES, (hp + 1) * LANES), slice(h * DV_A, (h + 1) * DV_A)
        qt = qt_ref[0, pair, :]
        km = k_ref[:, pair] * _head_lane_mask(hh)
        v_aug = jnp.concatenate([vt_ref[0, vs, :], ones_rows], axis=0)
        state = state_sc[h]
        m_old = m_sc[h]
        dm = _dot(lhs_t, jnp.where(rhs_head == h, rhs_all, 0.0).astype(BF16))
        dm = jnp.where(causal, dm, -jnp.inf)
        a_row = bcum[h:h + 1] + m_old
        mt = jnp.maximum(a_row, jnp.max(dm, axis=0, keepdims=True))
        w_t = jnp.exp(dm - mt) * (_dot(km, qt) * scale)
        numden = jnp.exp(a_row - mt) * _dot(state.astype(BF16), qt) + _dot(v_aug, w_t.astype(BF16))
        hc = numden[:DV_A] / jnp.maximum(jnp.abs(numden[DV_A:DV_A + 1]), jnp.exp(-mt))
        hn = hc * lax.rsqrt(jnp.mean(hc * hc, axis=0, keepdims=True) + EPS)
        h_ref[:, vs] = (hn * gn_ref[vs, :] * _sigmoid(ot_ref[0, vs, :])).T.astype(h_ref.dtype)
        g_row = btot[h:h + 1] + r[h:h + 1]
        m_new = jnp.maximum(btot[h:h + 1] + m_old, jnp.max(g_row, axis=1, keepdims=True))
        decay = jnp.exp(btot[h:h + 1] + m_old - m_new)
        v_w = (v_aug.astype(F32) * (jnp.exp(g_row - m_new) * scale)).astype(BF16)
        state_sc[h] = decay * state + _dot(v_w, km)
        m_sc[h] = m_new

    @pl.when(pl.program_id(1) == pl.num_programs(1) - 1)
    def _():
        for h in range(NH_A):
            qk = slice((h % HEADS_PER_PAIR) * DQK_A, (h % HEADS_PER_PAIR + 1) * DQK_A)
            state = state_sc[h]
            c_ref[0, h] = state[:DV_A].T[qk]
            n_ref[0, h:h + 1, :] = state[DV_A:DV_A + 1, qk]
            m_ref[0, :, h:h + 1] = m_sc[h][:, :1]


def _mlstm_prompt(qt, k, vt, ot, gates_t, b_gates, g_norm, batch):
    m = k.shape[0]
    L = MLSTM_TILE
    nc = m // batch // L
    rows = lambda b, c: (b * nc + c, 0)
    cols = lambda b, c: (b, 0, c)
    fixed = lambda b, c: (0, 0)
    return pl.pallas_call(
        _mlstm_prompt_kernel,
        grid=(batch, nc),
        in_specs=[
            pl.BlockSpec((1, NQK_A, L), cols), pl.BlockSpec((L, NQK_A), rows), pl.BlockSpec((1, NV_A, L), cols),
            pl.BlockSpec((1, NV_A, L), cols), pl.BlockSpec((1, 2 * NH_A, L), cols),
            pl.BlockSpec((2 * NH_A, 1), fixed), pl.BlockSpec((NV_A, L), fixed),
        ],
        out_specs=[
            pl.BlockSpec((L, NV_A), rows),
            pl.BlockSpec((1, NH_A, DQK_A, DV_A), lambda b, c: (b, 0, 0, 0)),
            pl.BlockSpec((1, NH_A, DQK_A), lambda b, c: (b, 0, 0)),
            pl.BlockSpec((1, 1, NH_A), lambda b, c: (b, 0, 0)),
        ],
        scratch_shapes=[pltpu.VMEM((NH_A, DV_A + 16, LANES), F32), pltpu.VMEM((NH_A, 1, MLSTM_TILE), F32)],
        out_shape=[
            jax.ShapeDtypeStruct((m, NV_A), BF16),
            jax.ShapeDtypeStruct((batch, NH_A, DQK_A, DV_A), F32),
            jax.ShapeDtypeStruct((batch, NH_A, DQK_A), F32),
            jax.ShapeDtypeStruct((batch, 1, NH_A), F32),
        ],
        compiler_params=_params(("parallel", "arbitrary"), 32), name="mlstm_prompt",
    )(qt, k, vt, ot, gates_t, b_gates, g_norm)


MLSTM_DEC_TILE = 8


def _mlstm_decode_kernel(q_ref, k_ref, v_ref, o_ref, gt_ref, bg_ref, gn_ref, c0_ref, n0_ref, m0_ref,
                         h_ref, c_ref, n_ref, m_ref):
    ig, lf = _mlstm_gates(gt_ref[...], bg_ref[...])
    scale = DQK_A ** -0.5
    q_t, k_t = q_ref[...].T, k_ref[...].T
    for j in range(MLSTM_DEC_TILE):
        for h in range(NH_A):
            qs, vs = slice(h * DQK_A, (h + 1) * DQK_A), slice(h * DV_A, (h + 1) * DV_A)
            q_row, k_row = q_ref[j:j + 1, qs], k_ref[j:j + 1, qs] * scale
            q_col, k_col = q_t[qs, j:j + 1], k_t[qs, j:j + 1] * scale
            v_row = v_ref[j:j + 1, vs]
            c_old, n_old, m_old = c0_ref[j, h], n0_ref[j, h:h + 1, :], m0_ref[j:j + 1, h:h + 1]
            i_g, a = ig[j:j + 1, h:h + 1], lf[j:j + 1, h:h + 1] + m_old
            mt = jnp.maximum(a, i_g)
            w_int = jnp.exp(a - mt)
            w = jnp.exp(i_g - mt) * jnp.sum(q_row * k_row, axis=1, keepdims=True)
            num = w_int * jnp.sum(q_col * c_old, axis=0, keepdims=True) + w * v_row
            den = w_int * jnp.sum(q_row * n_old, axis=1, keepdims=True) + w
            hc = num / jnp.maximum(jnp.abs(den), jnp.exp(-mt))
            h_ref[j:j + 1, vs] = _mlstm_head_out(hc, gn_ref[:, vs], o_ref[j:j + 1, vs])
            m_new = jnp.maximum(a, i_g)
            decay = jnp.exp(a - m_new)
            ws = jnp.exp(i_g - m_new)
            c_ref[j, h] = decay * c_old + (ws * k_col) * v_row
            n_ref[j, h:h + 1, :] = decay * n_old + ws * k_row
            m_ref[j:j + 1, h:h + 1] = m_new


def _mlstm_decode(q, k, v, o, gates, b_gates, g_norm, c0, n0, m0):
    nb = q.shape[0]
    tb = MLSTM_DEC_TILE
    rows = lambda i: (i, 0)
    fixed = lambda i: (0, 0)
    return pl.pallas_call(
        _mlstm_decode_kernel,
        grid=(nb // tb,),
        in_specs=[
            pl.BlockSpec((tb, NQK_A), rows), pl.BlockSpec((tb, NQK_A), rows), pl.BlockSpec((tb, NV_A), rows),
            pl.BlockSpec((tb, NV_A), rows), pl.BlockSpec((tb, 2 * NH_A), rows),
            pl.BlockSpec((1, 2 * NH_A), fixed), pl.BlockSpec((1, NV_A), fixed),
            pl.BlockSpec((tb, NH_A, DQK_A, DV_A), lambda i: (i, 0, 0, 0)),
            pl.BlockSpec((tb, NH_A, DQK_A), lambda i: (i, 0, 0)),
            pl.BlockSpec((tb, NH_A), rows),
        ],
        out_specs=[
            pl.BlockSpec((tb, NV_A), rows),
            pl.BlockSpec((tb, NH_A, DQK_A, DV_A), lambda i: (i, 0, 0, 0)),
            pl.BlockSpec((tb, NH_A, DQK_A), lambda i: (i, 0, 0)),
            pl.BlockSpec((tb, NH_A), rows),
        ],
        out_shape=[
            jax.ShapeDtypeStruct((nb, NV_A), F32),
            jax.ShapeDtypeStruct(c0.shape, F32),
            jax.ShapeDtypeStruct(n0.shape, F32),
            jax.ShapeDtypeStruct(m0.shape, F32),
        ],
        compiler_params=_params(("parallel",), 32), name="mlstm_decode",
    )(q, k, v, o, gates, b_gates, g_norm, c0, n0, m0)


def _fox_gates_kernel(fg_ref, bf_ref, lft_ref, qb_ref, kb_ref, d_sc):
    t = fg_ref.shape[0]
    lf = _log_sigmoid(fg_ref[...] + bf_ref[...])
    lft_ref[0] = lf.T
    tril = (_iota2((LANES, LANES), 0) >= _iota2((LANES, LANES), 1)).astype(F32)
    carry = jnp.zeros((1, N_HEADS), F32)
    for blk in range(t // LANES):
        rows = slice(blk * LANES, (blk + 1) * LANES)
        d = _dot_exact(tril, lf[rows]) + carry
        d_sc[rows, :] = d
        carry = d[LANES - 1:LANES, :]
    parts = _split3(d_sc[...] * LOG2E)
    head, lane = _iota2((N_HEADS, LANES), 0), _iota2((N_HEADS, LANES), 1)
    lane1 = _iota2((1, LANES), 1) % BIAS_LANES
    qb = jnp.where((lane1 >= 3) & (lane1 < 6), 1.0, 0.0)
    kb = jnp.where(lane1 < 3, 1.0, 0.0)
    for c, part in enumerate(parts):
        qb = qb + _dot(part, jnp.where(lane == head * BIAS_LANES + c, 1.0, 0.0).astype(BF16))
        kb = kb + _dot(part, jnp.where(lane == head * BIAS_LANES + 3 + c, -1.0, 0.0).astype(BF16))
    qb_ref[...] = qb.astype(BF16)
    kb_ref[...] = kb.astype(BF16)


def _fox_gates(fg, b_f, batch):
    m = fg.shape[0]
    t = m // batch
    return pl.pallas_call(
        _fox_gates_kernel,
        grid=(batch,),
        in_specs=[pl.BlockSpec((t, N_HEADS), lambda b: (b, 0)), pl.BlockSpec((1, N_HEADS), lambda b: (0, 0))],
        out_specs=[pl.BlockSpec((1, N_HEADS, t), lambda b: (b, 0, 0)),
                   pl.BlockSpec((t, LANES), lambda b: (b, 0)), pl.BlockSpec((t, LANES), lambda b: (b, 0))],
        out_shape=[jax.ShapeDtypeStruct((batch, N_HEADS, t), F32),
                   jax.ShapeDtypeStruct((m, LANES), BF16), jax.ShapeDtypeStruct((m, LANES), BF16)],
        scratch_shapes=[pltpu.VMEM((t, N_HEADS), F32)],
        compiler_params=_params(("parallel",), 32), name="fox_gates",
    )(fg, b_f)


def _head_lane_mask(hh):
    lane = _iota2((1, LANES), 1)
    return jnp.where((lane >= hh * D_HEAD) & (lane < (hh + 1) * D_HEAD), 1.0, 0.0).astype(BF16)


def _fox_prompt_kernel(q_ref, k_ref, vt_ref, qb_ref, kb_ref, o_ref, m_sc, acc_sc):
    qi, ki = pl.program_id(1), pl.program_id(2)
    tq, tk = q_ref.shape[0], k_ref.shape[0]
    n_pairs = N_HEADS // HEADS_PER_PAIR

    @pl.when(ki == 0)
    def _():
        m_sc[...] = jnp.full_like(m_sc, -jnp.inf)
        acc_sc[...] = jnp.zeros_like(acc_sc)

    def block(diagonal):
        qb, kb = qb_ref[...], kb_ref[...]
        ones = jnp.ones((D_HEAD, tk), BF16)
        bias_head = _iota2((1, LANES), 1) // BIAS_LANES
        visible = _iota2((tk, tq), 0) <= _iota2((tk, tq), 1)

        def pair(hp, carry):
            off = pl.multiple_of(hp * LANES, LANES)
            rhs = jnp.concatenate([q_ref[:, pl.ds(off, LANES)], qb], axis=1)
            kp = k_ref[:, pl.ds(off, LANES)]
            vtp = vt_ref[0, pl.ds(off, LANES), :]
            for hh in range(HEADS_PER_PAIR):
                head = hp * HEADS_PER_PAIR + hh
                kb_h = kb * jnp.where(bias_head == head, 1.0, 0.0).astype(BF16)
                s = _dot_nt(jnp.concatenate([kp * _head_lane_mask(hh), kb_h], axis=1), rhs)
                if diagonal:
                    s = jnp.where(visible, s, -jnp.inf)
                m_old = m_sc[head]
                m_new = jnp.maximum(m_old, jnp.max(s, axis=0, keepdims=True))
                p = jnp.exp2(s - m_new).astype(BF16)
                v_aug = (jnp.concatenate([vtp[:D_HEAD], ones], axis=0) if hh == 0
                         else jnp.concatenate([ones, vtp[D_HEAD:]], axis=0))
                acc_sc[head] = jnp.exp2(m_old - m_new) * acc_sc[head] + _dot(v_aug, p)
                m_sc[head] = m_new
            return carry

        lax.fori_loop(0, n_pairs, pair, 0)

    @pl.when(ki < qi)
    def _():
        block(False)

    @pl.when(ki == qi)
    def _():
        block(True)

    @pl.when(ki == pl.num_programs(2) - 1)
    def _():
        def pair(hp, carry):
            off = pl.multiple_of(hp * LANES, LANES)
            a0, a1 = acc_sc[hp * HEADS_PER_PAIR], acc_sc[hp * HEADS_PER_PAIR + 1]
            o_t = jnp.concatenate([a0[:D_HEAD] * (1.0 / a0[D_HEAD:D_HEAD + 1]),
                                   a1[D_HEAD:] * (1.0 / a1[:1])], axis=0)
            o_ref[:, pl.ds(off, LANES)] = o_t.T.astype(o_ref.dtype)
            return carry

        lax.fori_loop(0, n_pairs, pair, 0)


def _fox_prompt(q, k, vt, qb, kb, batch):
    m = q.shape[0]
    t = m // batch
    tile = ATT_TILE
    n = t // tile
    qmap = lambda b, qi, ki: (b * n + qi, 0)
    kmap = lambda b, qi, ki: (b * n + jnp.minimum(ki, qi), 0)
    return pl.pallas_call(
        _fox_prompt_kernel,
        grid=(batch, n, n),
        in_specs=[
            pl.BlockSpec((tile, D_MODEL), qmap), pl.BlockSpec((tile, D_MODEL), kmap),
            pl.BlockSpec((1, D_MODEL, tile), lambda b, qi, ki: (b, 0, jnp.minimum(ki, qi))),
            pl.BlockSpec((tile, LANES), qmap), pl.BlockSpec((tile, LANES), kmap),
        ],
        out_specs=pl.BlockSpec((tile, D_MODEL), qmap),
        out_shape=jax.ShapeDtypeStruct((m, D_MODEL), BF16),
        scratch_shapes=[pltpu.VMEM((N_HEADS, 1, tile), F32), pltpu.VMEM((N_HEADS, LANES, tile), F32)],
        compiler_params=_params(("parallel", "parallel", "arbitrary"), 40), name="fox_prompt",
    )(q, k, vt, qb, kb)


SUFFIX_ROWS = SB_SUB + 16


def _suffix_lhs():
    r, c = _iota2((SUFFIX_ROWS, SB_SUB), 0), _iota2((SUFFIX_ROWS, SB_SUB), 1)
    u = jnp.where((c > r) | (r >= SB_SUB), 1.0, 0.0).astype(BF16)
    return jnp.concatenate([u, u], axis=1)


def _sb_prompt_kernel(q_ref, k_ref, vt_ref, o_ref, carry_sc, acc_sc):
    qi, kk = pl.program_id(1), pl.program_id(2)
    tq, tk = q_ref.shape[0], k_ref.shape[0]
    n_pairs = N_HEADS // HEADS_PER_PAIR
    n_sub = tk // SB_SUB

    @pl.when(kk == 0)
    def _():
        carry_sc[...] = jnp.zeros_like(carry_sc)
        acc_sc[...] = jnp.zeros_like(acc_sc)

    def block(diagonal):
        suffix_lhs = _suffix_lhs()
        zeros = jnp.zeros((D_HEAD, tk), BF16)
        visible = _iota2((tk, tq), 0) < _iota2((tk, tq), 1)

        def pair(hp, carry):
            off = pl.multiple_of(hp * LANES, LANES)
            qp = q_ref[:, pl.ds(off, LANES)]
            kp = k_ref[:, pl.ds(off, LANES)]
            vtp = vt_ref[0, pl.ds(off, LANES), :]
            for hh in range(HEADS_PER_PAIR):
                head = hp * HEADS_PER_PAIR + hh
                z = _dot_nt(kp * _head_lane_mask(hh), qp)
                nl = jnp.maximum(z, 0.0) + jnp.log2(1.0 + jnp.exp2(-jnp.abs(z)))
                if diagonal:
                    nl = jnp.where(visible, nl, 0.0)
                hi = nl.astype(BF16)
                lo = (nl - hi.astype(F32)).astype(BF16)
                newer = carry_sc[head]
                later = [None] * n_sub
                for sub in reversed(range(n_sub)):
                    keys = slice(sub * SB_SUB, (sub + 1) * SB_SUB)
                    sums = _dot(suffix_lhs, jnp.concatenate([hi[keys], lo[keys]], axis=0))
                    later[sub] = sums[:SB_SUB] + newer
                    newer = newer + sums[SB_SUB:SB_SUB + 1]
                carry_sc[head] = newer
                a = jnp.exp2(z - nl - jnp.concatenate(later, axis=0))
                if diagonal:
                    a = jnp.where(visible, a, 0.0)
                v_h = (jnp.concatenate([vtp[:D_HEAD], zeros], axis=0) if hh == 0
                       else jnp.concatenate([zeros, vtp[D_HEAD:]], axis=0))
                acc_sc[hp] += _dot(v_h, a.astype(BF16))
            return carry

        lax.fori_loop(0, n_pairs, pair, 0)

    @pl.when(kk == 0)
    def _():
        block(True)

    @pl.when((kk > 0) & (kk <= qi))
    def _():
        block(False)

    @pl.when(kk == pl.num_programs(2) - 1)
    def _():
        def pair(hp, carry):
            off = pl.multiple_of(hp * LANES, LANES)
            o_ref[:, pl.ds(off, LANES)] = acc_sc[hp].T.astype(o_ref.dtype)
            return carry

        lax.fori_loop(0, n_pairs, pair, 0)


def _sb_prompt(q, k, vt, batch):
    m = q.shape[0]
    t = m // batch
    tile = ATT_TILE
    n = t // tile
    qmap = lambda b, qi, kk: (b * n + qi, 0)
    return pl.pallas_call(
        _sb_prompt_kernel,
        grid=(batch, n, n),
        in_specs=[
            pl.BlockSpec((tile, D_MODEL), qmap),
            pl.BlockSpec((tile, D_MODEL), lambda b, qi, kk: (b * n + jnp.maximum(qi - kk, 0), 0)),
            pl.BlockSpec((1, D_MODEL, tile), lambda b, qi, kk: (b, 0, jnp.maximum(qi - kk, 0))),
        ],
        out_specs=pl.BlockSpec((tile, D_MODEL), qmap),
        out_shape=jax.ShapeDtypeStruct((m, D_MODEL), BF16),
        scratch_shapes=[pltpu.VMEM((N_HEADS, 1, tile), F32),
                        pltpu.VMEM((N_HEADS // HEADS_PER_PAIR, LANES, tile), F32)],
        compiler_params=_params(("parallel", "parallel", "arbitrary"), 40), name="sb_prompt",
    )(q, k, vt)


def _page_spec(block, n_pages, slot):
    p = PAGES_PER_STEP
    return pl.BlockSpec(block, lambda b, j, pt: (pt[b, n_pages - 1 - (j * p + slot)],) + (0,) * (len(block) - 1))


def _load_query_columns(q_row, qcol_sc):
    for h in range(N_HEADS):
        qcol_sc[h] = jnp.broadcast_to(q_row[:, h * D_HEAD:(h + 1) * D_HEAD].T, (D_HEAD, PAGE_SIZE))


def _page_scores(k_refs, qcol_sc):
    q = qcol_sc[...]
    return jnp.concatenate([jnp.sum(k_ref[0] * q, axis=1) for k_ref in k_refs], axis=1)


def _page_suffix_sums(x, carry):
    p = x.shape[1] // PAGE_SIZE
    r, c = _iota2((PAGE_SIZE, 2 * PAGE_SIZE), 0), _iota2((PAGE_SIZE, 2 * PAGE_SIZE), 1)
    u = jnp.where((r > c) | (c >= PAGE_SIZE), 1.0, 0.0).astype(BF16)
    stacked = jnp.concatenate([x[:, i * PAGE_SIZE:(i + 1) * PAGE_SIZE] for i in range(p)], axis=0)
    sums = sum(_dot(part, u) for part in _split3(stacked))
    out = []
    for i in range(p):
        rows = slice(i * N_HEADS, (i + 1) * N_HEADS)
        out.append(sums[rows, :PAGE_SIZE] + carry)
        carry = carry + sums[rows, PAGE_SIZE:]
    return jnp.concatenate(out, axis=1), carry


def _accumulate_values(v_refs, w_sc, acc_sc, rescale):
    for h in range(N_HEADS):
        acc = acc_sc[h]
        if rescale is not None:
            acc = acc * rescale[h:h + 1, :]
        for i, v_ref in enumerate(v_refs):
            acc = acc + w_sc[h:h + 1, i * PAGE_SIZE:(i + 1) * PAGE_SIZE] * v_ref[0, h]
        acc_sc[h] = acc


def _store_head_sums(acc_sc, inv, o_ref):
    ones = jnp.ones((8, PAGE_SIZE), BF16)
    for h in range(N_HEADS):
        row = sum(_dot_nt(ones, part) for part in _split3(acc_sc[h]))[:1]
        if inv is not None:
            row = row * inv[h:h + 1, :]
        o_ref[0, :, h * D_HEAD:(h + 1) * D_HEAD] = row


def _fox_decode_kernel(pt_ref, q_ref, kn_ref, vn_ref, lfn_ref, *refs):
    p = PAGES_PER_STEP
    k_refs, v_refs, lf_refs = refs[:p], refs[p:2 * p], refs[2 * p:3 * p]
    o_ref, qcol_sc, w_sc, m_sc, l_sc, carry_sc, acc_sc = refs[3 * p:]
    j = pl.program_id(1)

    @pl.when(j == 0)
    def _():
        q = q_ref[0] * (D_HEAD ** -0.5)
        _load_query_columns(q, qcol_sc)
        seg = (_iota2((D_MODEL, N_HEADS), 0) // D_HEAD == _iota2((D_MODEL, N_HEADS), 1)).astype(F32)
        m_sc[...] = _dot_exact(q * kn_ref[0], seg).T
        l_sc[...] = jnp.ones_like(l_sc)
        lane0 = _iota2((D_HEAD, PAGE_SIZE), 1) == 0
        for h in range(N_HEADS):
            acc_sc[h] = jnp.where(lane0, vn_ref[0][:, h * D_HEAD:(h + 1) * D_HEAD].T, 0.0)
        carry_sc[...] = jnp.broadcast_to(lfn_ref[0].T, carry_sc.shape)

    logf = jnp.concatenate([lf_ref[0] for lf_ref in lf_refs], axis=1)
    bias, carry = _page_suffix_sums(logf, carry_sc[...])
    carry_sc[...] = carry
    s = _page_scores(k_refs, qcol_sc) + bias
    m_old = m_sc[...]
    m_new = jnp.maximum(m_old, jnp.max(s, axis=1, keepdims=True))
    alpha = jnp.exp(m_old - m_new)
    w = jnp.exp(s - m_new)
    w_sc[...] = w
    l_sc[...] = alpha * l_sc[...] + jnp.sum(w, axis=1, keepdims=True)
    m_sc[...] = m_new
    _accumulate_values(v_refs, w_sc, acc_sc, alpha)

    @pl.when(j == pl.num_programs(1) - 1)
    def _():
        _store_head_sums(acc_sc, 1.0 / l_sc[...], o_ref)


def _decode_scratch():
    p = PAGES_PER_STEP
    return [pltpu.VMEM((N_HEADS, D_HEAD, PAGE_SIZE), F32), pltpu.VMEM((N_HEADS, p * PAGE_SIZE), F32)]


def _fox_decode(q, k_new, v_new, lf_new, cache_k, cache_v, cache_lf, page_table):
    nb, n_pages = page_table.shape
    p = PAGES_PER_STEP
    row = lambda b, j, pt: (b, 0, 0)
    in_specs = [pl.BlockSpec((1, 1, D_MODEL), row)] * 3 + [pl.BlockSpec((1, 1, N_HEADS), row)]
    in_specs += [_page_spec((1, N_HEADS, D_HEAD, PAGE_SIZE), n_pages, i) for i in range(p)] * 2
    in_specs += [_page_spec((1, N_HEADS, PAGE_SIZE), n_pages, i) for i in range(p)]
    grid_spec = pltpu.PrefetchScalarGridSpec(
        num_scalar_prefetch=1, grid=(nb, n_pages // p), in_specs=in_specs,
        out_specs=pl.BlockSpec((1, 1, D_MODEL), row),
        scratch_shapes=_decode_scratch() + [
            pltpu.VMEM((N_HEADS, 1), F32), pltpu.VMEM((N_HEADS, 1), F32), pltpu.VMEM((N_HEADS, PAGE_SIZE), F32),
            pltpu.VMEM((N_HEADS, D_HEAD, PAGE_SIZE), F32)])
    return pl.pallas_call(
        _fox_decode_kernel, grid_spec=grid_spec,
        out_shape=jax.ShapeDtypeStruct((nb, 1, D_MODEL), F32),
        compiler_params=_params(("parallel", "arbitrary"), 40), name="fox_decode",
    )(page_table, q, k_new, v_new, lf_new, *([cache_k] * p), *([cache_v] * p), *([cache_lf] * p))


def _sb_decode_kernel(pt_ref, q_ref, *refs):
    p = PAGES_PER_STEP
    k_refs, v_refs = refs[:p], refs[p:2 * p]
    o_ref, qcol_sc, w_sc, carry_sc, acc_sc = refs[2 * p:]
    j = pl.program_id(1)

    @pl.when(j == 0)
    def _():
        _load_query_columns(q_ref[0] * (D_HEAD ** -0.5), qcol_sc)
        carry_sc[...] = jnp.zeros_like(carry_sc)
        acc_sc[...] = jnp.zeros_like(acc_sc)

    z = _page_scores(k_refs, qcol_sc)
    nl = _softplus(z)
    later, carry = _page_suffix_sums(nl, carry_sc[...])
    carry_sc[...] = carry
    w_sc[...] = jnp.exp(z - nl - later)
    _accumulate_values(v_refs, w_sc, acc_sc, None)

    @pl.when(j == pl.num_programs(1) - 1)
    def _():
        _store_head_sums(acc_sc, None, o_ref)


def _sb_decode(q, cache_k, cache_v, page_table):
    nb, n_pages = page_table.shape
    p = PAGES_PER_STEP
    row = lambda b, j, pt: (b, 0, 0)
    in_specs = [pl.BlockSpec((1, 1, D_MODEL), row)]
    in_specs += [_page_spec((1, N_HEADS, D_HEAD, PAGE_SIZE), n_pages, i) for i in range(p)] * 2
    grid_spec = pltpu.PrefetchScalarGridSpec(
        num_scalar_prefetch=1, grid=(nb, n_pages // p), in_specs=in_specs,
        out_specs=pl.BlockSpec((1, 1, D_MODEL), row),
        scratch_shapes=_decode_scratch() + [
            pltpu.VMEM((N_HEADS, PAGE_SIZE), F32), pltpu.VMEM((N_HEADS, D_HEAD, PAGE_SIZE), F32)])
    return pl.pallas_call(
        _sb_decode_kernel, grid_spec=grid_spec,
        out_shape=jax.ShapeDtypeStruct((nb, 1, D_MODEL), F32),
        compiler_params=_params(("parallel", "arbitrary"), 40), name="sb_decode",
    )(page_table, q, *([cache_k] * p), *([cache_v] * p))


def _fox_logf_kernel(fg_ref, bf_ref, lf_ref):
    lf_ref[...] = _log_sigmoid(fg_ref[...] + bf_ref[...])


def _fox_logf(fg, b_f):
    return pl.pallas_call(_fox_logf_kernel, out_shape=jax.ShapeDtypeStruct(fg.shape, F32), name="fox_logf")(fg, b_f)


def _heads_last(x_t, batch, seq):
    return x_t.reshape(batch, N_HEADS, D_HEAD, seq).transpose(0, 3, 1, 2)


def _pages_token_minor(cache):
    return cache.transpose(0, 2, 3, 1)


def kernel(x_prompt, x_sample, state_mlstm_c, state_mlstm_n, state_mlstm_m, cache_fox_k, cache_fox_v, cache_fox_logf, cache_sb_k, cache_sb_v, page_table, norm_mix, norm_ffn, norm_final, mlstm_w_in, mlstm_b_gates, mlstm_norm, mlstm_w_out, fox_w_in, fox_b_f, fox_w_out, sb_w_in, sb_w_out, ffn_w_gu, ffn_w_down):
    batch, seq, d = x_prompt.shape
    nb = x_sample.shape[0]
    xp = x_prompt.reshape(batch * seq, d)
    xs = x_sample.reshape(nb, d)
    q_scale = D_HEAD ** -0.5 * LOG2E
    out = {name: [] for name in ("pmc", "pmn", "pmm", "smc", "smn", "smm", "pfk", "pfv", "pfl", "sfk", "sfv", "sfl",
                                 "psk", "psv", "ssk", "ssv")}
    for i in range(DEPTH):
        j = i // N_MIXERS
        g_mix = norm_mix[i].reshape(1, d)
        if i % N_MIXERS == 0:
            w_in = mlstm_w_in[j].astype(BF16)
            splits = (NQK_A, 2 * NQK_A, 2 * NQK_A + NV_A, 2 * NQK_A + 2 * NV_A)
            ws = [w_in[:, a:b] for a, b in zip((0,) + splits, splits + (w_in.shape[1],))]
            b_gates = mlstm_b_gates[j].reshape(1, 2 * NH_A)
            g_norm = mlstm_norm[j].reshape(1, NV_A)
            w_out = mlstm_w_out[j].astype(BF16)
            plan = ((True, 1.0, (BF16,)), (False, 1.0, (BF16,)), (True, 1.0, (BF16,)), (True, 1.0, (F32,)),
                    (True, 1.0, (F32,)))
            qt, k, vt, ot, gt_t = _proj(xp, g_mix, [ws[0].T, ws[1], ws[2].T, ws[3].T, ws[4].T], plan, batch)
            ap, c1, n1, m1 = _mlstm_prompt(qt, k, vt, ot, gt_t, b_gates.reshape(2 * NH_A, 1),
                                           jnp.broadcast_to(g_norm.reshape(NV_A, 1), (NV_A, MLSTM_TILE)), batch)
            q, k, v, o, gt = _proj(xs, g_mix, ws, ((False, 1.0, (F32,)),) * 5, nb)
            a_s, c2, n2, m2 = _mlstm_decode(q, k, v, o, gt, b_gates, g_norm,
                                            state_mlstm_c[j], state_mlstm_n[j], state_mlstm_m[j])
            out["pmc"].append(c1); out["pmn"].append(n1); out["pmm"].append(m1.reshape(batch, NH_A))
            out["smc"].append(c2); out["smn"].append(n2); out["smm"].append(m2)
        elif i % N_MIXERS == 1:
            w_in = fox_w_in[j].astype(BF16)
            wq, wk, wv, wf = (w_in[:, a:b] for a, b in ((0, d), (d, 2 * d), (2 * d, 3 * d), (3 * d, 3 * d + N_HEADS)))
            b_f = fox_b_f[j].reshape(1, N_HEADS)
            w_out = fox_w_out[j].astype(BF16)
            plan = ((False, q_scale, (BF16,)), (False, 1.0, (BF16,)), (True, 1.0, (F32,)), (True, 1.0, (F32, BF16)),
                    (False, 1.0, (F32,)))
            q, kb16, k1t, v1t, vt16, fg = _proj(xp, g_mix, [wq, wk, wk.T, wv.T, wf], plan, batch)
            l1t, qbias, kbias = _fox_gates(fg, b_f, batch)
            ap = _fox_prompt(q, kb16, vt16, qbias, kbias, batch)
            q, k2, v2, fg = _proj(xs, g_mix, [wq, wk, wv, wf], ((False, 1.0, (F32,)),) * 4, nb)
            l2 = _fox_logf(fg, b_f)
            a_s = _fox_decode(q.reshape(nb, 1, d), k2.reshape(nb, 1, d), v2.reshape(nb, 1, d),
                              l2.reshape(nb, 1, N_HEADS),
                              _pages_token_minor(cache_fox_k[j]), _pages_token_minor(cache_fox_v[j]),
                              cache_fox_logf[j].transpose(0, 2, 1), page_table).reshape(nb, d)
            out["pfk"].append(_heads_last(k1t, batch, seq)); out["pfv"].append(_heads_last(v1t, batch, seq))
            out["pfl"].append(l1t.transpose(0, 2, 1))
            out["sfk"].append(k2.reshape(nb, 1, N_HEADS, D_HEAD)); out["sfv"].append(v2.reshape(nb, 1, N_HEADS, D_HEAD))
            out["sfl"].append(l2.reshape(nb, 1, N_HEADS))
        else:
            w_in = sb_w_in[j].astype(BF16)
            wq, wk, wv = (w_in[:, a:b] for a, b in ((0, d), (d, 2 * d), (2 * d, 3 * d)))
            w_out = sb_w_out[j].astype(BF16)
            plan = ((False, q_scale, (BF16,)), (False, 1.0, (BF16,)), (True, 1.0, (F32,)), (True, 1.0, (F32, BF16)))
            q, kb16, k1t, v1t, vt16 = _proj(xp, g_mix, [wq, wk, wk.T, wv.T], plan, batch)
            ap = _sb_prompt(q, kb16, vt16, batch)
            q, k2, v2 = _proj(xs, g_mix, [wq, wk, wv], ((False, 1.0, (F32,)),) * 3, nb)
            a_s = _sb_decode(q.reshape(nb, 1, d), _pages_token_minor(cache_sb_k[j]), _pages_token_minor(cache_sb_v[j]),
                             page_table).reshape(nb, d)
            out["psk"].append(_heads_last(k1t, batch, seq)); out["psv"].append(_heads_last(v1t, batch, seq))
            out["ssk"].append(k2.reshape(nb, 1, N_HEADS, D_HEAD)); out["ssv"].append(v2.reshape(nb, 1, N_HEADS, D_HEAD))
        g_ffn = norm_ffn[i].reshape(1, d)
        w_gu = ffn_w_gu[i].astype(BF16)
        w_down = ffn_w_down[i].astype(BF16)
        xp = _out_ffn(xp, ap, w_out, g_ffn, w_gu, w_down)
        xs = _out_ffn(xs, a_s, w_out, g_ffn, w_gu, w_down)
    g_fin = norm_final.reshape(1, d)
    y_prompt = _final_norm(xp, g_fin).reshape(batch, seq, d)
    y_sample = _final_norm(xs, g_fin).reshape(nb, 1, d)
    st = {name: jnp.stack(vals) for name, vals in out.items()}
    return (y_prompt, y_sample, st["pmc"], st["pmn"], st["pmm"], st["smc"], st["smn"], st["smm"],
            st["pfk"], st["pfv"], st["pfl"], st["sfk"], st["sfv"], st["sfl"],
            st["psk"], st["psv"], st["ssk"], st["ssv"])
```

```python
import functools

import jax
import jax.numpy as jnp
from jax import lax
from jax.experimental import pallas as pl
from jax.experimental.pallas import tpu as pltpu

F32 = jnp.float32
BF16 = jnp.bfloat16

D_MODEL = 1024
DEPTH = 4
N_MIXERS = 3
NH_A = 8
DV_A = D_MODEL // NH_A
DQK_A = DV_A // 2
NQK_A = NH_A * DQK_A
NV_A = NH_A * DV_A
MLSTM_CHUNK = 64
GATE_CAP = 15.0
N_HEADS = 16
D_HEAD = D_MODEL // N_HEADS
PAGE_SIZE = 128
D_FF = ((8 * D_MODEL // 3 + 255) // 256) * 256
EPS = 1e-6
LOG2E = 1.4426950408889634

LANES = 128
MIB = 1024 * 1024

ROW_TILE = 512
FFN_ROW_TILE = 512
FFN_COL_TILE = 1408
ATT_TILE = 512
PAIR_UNROLL = 2
SB_SUB = 128
PAGES_PER_STEP = 8
HEADS_PER_PAIR = LANES // D_HEAD
BIAS_LANES = LANES // N_HEADS

NT_DIMS = (((1,), (1,)), ((), ()))
TN_DIMS = (((0,), (0,)), ((), ()))


def _params(semantics, vmem_mib):
    return pltpu.CompilerParams(dimension_semantics=semantics, vmem_limit_bytes=vmem_mib * MIB)


def _dot(a, b):
    return jnp.dot(a, b, preferred_element_type=F32)


def _dot_nt(a, b):
    return lax.dot_general(a, b, NT_DIMS, preferred_element_type=F32)


def _dot_exact(a, b):
    return jnp.dot(a, b, precision=lax.Precision.HIGHEST, preferred_element_type=F32)


def _split3(x):
    hi = x.astype(BF16)
    r = x - hi.astype(F32)
    mid = r.astype(BF16)
    return hi, mid, (r - mid.astype(F32)).astype(BF16)


def _rms(x, g):
    return x * lax.rsqrt(jnp.mean(x * x, axis=-1, keepdims=True) + EPS) * g


def _softplus(x):
    return jnp.maximum(x, 0.0) + jnp.log1p(jnp.exp(-jnp.abs(x)))


def _log_sigmoid(x):
    return -_softplus(-x)


def _sigmoid(x):
    return 1.0 / (1.0 + jnp.exp(-x))


def _iota2(shape, dim):
    return lax.broadcasted_iota(jnp.int32, shape, dim)


def _proj_kernel(x_ref, g_ref, *refs, plan):
    n_w = len(plan)
    w_refs, o_refs = refs[:n_w], refs[n_w:]
    h = _rms(x_ref[...], g_ref[...]).astype(BF16)
    k = 0
    for w_ref, (transposed, scale, dts) in zip(w_refs, plan):
        y = _dot_nt(w_ref[...], h) if transposed else _dot(h, w_ref[...])
        if scale != 1.0:
            y = y * scale
        for dt in dts:
            if transposed:
                o_refs[k][0] = y.astype(dt)
            else:
                o_refs[k][...] = y.astype(dt)
            k += 1


def _proj(x, g, ws, plan, batch):
    m = x.shape[0]
    tm = min(m, ROW_TILE)
    tiles_per_seq = m // batch // tm if any(p[0] for p in plan) else 1
    in_specs = [pl.BlockSpec((tm, D_MODEL), lambda i: (i, 0)), pl.BlockSpec((1, D_MODEL), lambda i: (0, 0))]
    in_specs += [pl.BlockSpec(w.shape, lambda i: (0, 0)) for w in ws]
    out_shape, out_specs = [], []
    for w, (transposed, _, dts) in zip(ws, plan):
        for dt in dts:
            if transposed:
                n = w.shape[0]
                out_shape.append(jax.ShapeDtypeStruct((batch, n, m // batch), dt))
                out_specs.append(pl.BlockSpec((1, n, tm), lambda i: (i // tiles_per_seq, 0, i % tiles_per_seq)))
            else:
                n = w.shape[1]
                out_shape.append(jax.ShapeDtypeStruct((m, n), dt))
                out_specs.append(pl.BlockSpec((tm, n), lambda i: (i, 0)))
    return pl.pallas_call(
        functools.partial(_proj_kernel, plan=plan),
        grid=(m // tm,), in_specs=in_specs, out_specs=out_specs, out_shape=out_shape,
        compiler_params=_params(("parallel",), 56), name="proj",
    )(x, g, *ws)


def _out_ffn_kernel(x_ref, a_ref, wo_ref, g_ref, wg_ref, wu_ref, wd_ref, *refs, final_norm):
    gf_ref = refs[0] if final_norm else None
    o_ref, h_sc = refs[-2:]
    f = pl.program_id(1)
    last = pl.num_programs(1) - 1

    @pl.when(f == 0)
    def _():
        xm = x_ref[...] + _dot(a_ref[...].astype(BF16), wo_ref[...])
        o_ref[...] = xm
        h_sc[...] = _rms(xm, g_ref[...]).astype(BF16)

    h = h_sc[...]
    gate = _dot(h, wg_ref[...])
    up = _dot(h, wu_ref[...])
    act = (gate * _sigmoid(gate) * up).astype(BF16)
    total = o_ref[...] + _dot(act, wd_ref[...])
    if final_norm:
        @pl.when(f < last)
        def _():
            o_ref[...] = total

        @pl.when(f == last)
        def _():
            o_ref[...] = _rms(total, gf_ref[...])
    else:
        o_ref[...] = total


def _out_ffn(x, a, wo, g, wgu, wd, g_final=None):
    m = x.shape[0]
    tm = min(m, FFN_ROW_TILE)
    nf = D_FF // FFN_COL_TILE
    gain = pl.BlockSpec((1, D_MODEL), lambda i, f: (0, 0))
    final = () if g_final is None else (g_final,)
    return pl.pallas_call(
        functools.partial(_out_ffn_kernel, final_norm=g_final is not None),
        grid=(m // tm, nf),
        in_specs=[
            pl.BlockSpec((tm, D_MODEL), lambda i, f: (i, 0)),
            pl.BlockSpec((tm, D_MODEL), lambda i, f: (i, 0)),
            pl.BlockSpec((D_MODEL, D_MODEL), lambda i, f: (0, 0)),
            gain,
            pl.BlockSpec((D_MODEL, FFN_COL_TILE), lambda i, f: (0, f)),
            pl.BlockSpec((D_MODEL, FFN_COL_TILE), lambda i, f: (0, nf + f)),
            pl.BlockSpec((FFN_COL_TILE, D_MODEL), lambda i, f: (f, 0)),
        ] + [gain] * len(final),
        out_specs=pl.BlockSpec((tm, D_MODEL), lambda i, f: (i, 0)),
        out_shape=jax.ShapeDtypeStruct((m, D_MODEL), F32),
        scratch_shapes=[pltpu.VMEM((tm, D_MODEL), BF16)],
        compiler_params=_params(("parallel", "arbitrary"), 48), name="out_ffn",
    )(x, a, wo, g, wgu, wgu, wd, *final)


MLSTM_TILE = LANES


def _mlstm_gates(pre, bias):
    gates = GATE_CAP * jnp.tanh((pre + bias) / GATE_CAP)
    return gates[:, :NH_A], _log_sigmoid(gates[:, NH_A:])


def _mlstm_prompt_kernel(qt_ref, k_ref, vt_ref, ot_ref, gt_ref, bg_ref, gn_ref, h_ref, c_ref, n_ref, m_ref,
                         state_sc, m_sc):
    L = MLSTM_TILE
    aug = DV_A + 16

    @pl.when(pl.program_id(1) == 0)
    def _():
        state_sc[...] = jnp.zeros_like(state_sc)
        m_sc[...] = jnp.zeros_like(m_sc)

    gates = GATE_CAP * jnp.tanh((gt_ref[0] + bg_ref[...]) / GATE_CAP)
    ig, lf = gates[:NH_A], _log_sigmoid(gates[NH_A:])
    r_i, c_i = _iota2((L, 2 * L), 0), _iota2((L, 2 * L), 1)
    prefix = jnp.where((r_i <= c_i) | (c_i >= L), 1.0, 0.0).astype(BF16)
    sums = sum(_dot(part, prefix) for part in _split3(lf))
    bcum, btot = sums[:, :L], sums[:, L:]
    r = ig - bcum
    ones8, zeros16 = jnp.ones((NH_A, L), F32), jnp.zeros((2 * NH_A, L), F32)
    r_parts, b_parts = _split3(r), _split3(bcum)
    lhs_t = jnp.concatenate([p.astype(F32) for p in r_parts] + [ones8] * 3 + [zeros16], axis=0).T.astype(BF16)
    rhs_all = jnp.concatenate([ones8] * 3 + [p.astype(F32) for p in b_parts] + [zeros16], axis=0)
    rhs_head = _iota2((8 * NH_A, 1), 0) % NH_A
    causal = _iota2((L, L), 0) <= _iota2((L, L), 1)
    ones_rows = jnp.ones((aug - DV_A, L), BF16)
    scale = DQK_A ** -0.5
    for h in range(NH_A):
        hp, hh = divmod(h, HEADS_PER_PAIR)
        pair, vs = slice(hp * LANES, (hp + 1) * LANES), slice(h * DV_A, (h + 1) * DV_A)
        qt = qt_ref[0, pair, :]
        km = k_ref[:, pair] * _head_lane_mask(hh)
        v_aug = jnp.concatenate([vt_ref[0, vs, :], ones_rows], axis=0)
        state = state_sc[h]
        m_old = m_sc[h]
        dm = _dot(lhs_t, jnp.where(rhs_head == h, rhs_all, 0.0).astype(BF16))
        dm = jnp.where(causal, dm, -jnp.inf)
        a_row = bcum[h:h + 1] + m_old
        mt = jnp.maximum(a_row, jnp.max(dm, axis=0, keepdims=True))
        w_t = jnp.exp(dm - mt) * (_dot(km, qt) * scale)
        numden = jnp.exp(a_row - mt) * _dot(state.astype(BF16), qt) + _dot(v_aug, w_t.astype(BF16))
        hc = numden[:DV_A] / jnp.maximum(jnp.abs(numden[DV_A:DV_A + 1]), jnp.exp(-mt))
        hn = hc * lax.rsqrt(jnp.mean(hc * hc, axis=0, keepdims=True) + EPS)
        h_ref[:, vs] = (hn * gn_ref[vs, :] * _sigmoid(ot_ref[0, vs, :])).T.astype(h_ref.dtype)
        g_row = btot[h:h + 1] + r[h:h + 1]
        m_new = jnp.maximum(btot[h:h + 1] + m_old, jnp.max(g_row, axis=1, keepdims=True))
        decay = jnp.exp(btot[h:h + 1] + m_old - m_new)
        v_w = (v_aug.astype(F32) * (jnp.exp(g_row - m_new) * scale)).astype(BF16)
        state_sc[h] = decay * state + _dot(v_w, km)
        m_sc[h] = m_new

    @pl.when(pl.program_id(1) == pl.num_programs(1) - 1)
    def _():
        for h in range(NH_A):
            qk = slice((h % HEADS_PER_PAIR) * DQK_A, (h % HEADS_PER_PAIR + 1) * DQK_A)
            state = state_sc[h]
            c_ref[0, h] = state[:DV_A].T[qk]
            n_ref[0, h:h + 1, :] = state[DV_A:DV_A + 1, qk]
            m_ref[0, :, h:h + 1] = m_sc[h][:, :1]


def _mlstm_prompt(qt, k, vt, ot, gates_t, b_gates, g_norm, batch):
    m = k.shape[0]
    L = MLSTM_TILE
    nc = m // batch // L
    rows = lambda b, c: (b * nc + c, 0)
    cols = lambda b, c: (b, 0, c)
    fixed = lambda b, c: (0, 0)
    return pl.pallas_call(
        _mlstm_prompt_kernel,
        grid=(batch, nc),
        in_specs=[
            pl.BlockSpec((1, NQK_A, L), cols), pl.BlockSpec((L, NQK_A), rows), pl.BlockSpec((1, NV_A, L), cols),
            pl.BlockSpec((1, NV_A, L), cols), pl.BlockSpec((1, 2 * NH_A, L), cols),
            pl.BlockSpec((2 * NH_A, 1), fixed), pl.BlockSpec((NV_A, L), fixed),
        ],
        out_specs=[
            pl.BlockSpec((L, NV_A), rows),
            pl.BlockSpec((1, NH_A, DQK_A, DV_A), lambda b, c: (b, 0, 0, 0)),
            pl.BlockSpec((1, NH_A, DQK_A), lambda b, c: (b, 0, 0)),
            pl.BlockSpec((1, 1, NH_A), lambda b, c: (b, 0, 0)),
        ],
        scratch_shapes=[pltpu.VMEM((NH_A, DV_A + 16, LANES), F32), pltpu.VMEM((NH_A, 1, MLSTM_TILE), F32)],
        out_shape=[
            jax.ShapeDtypeStruct((m, NV_A), BF16),
            jax.ShapeDtypeStruct((batch, NH_A, DQK_A, DV_A), F32),
            jax.ShapeDtypeStruct((batch, NH_A, DQK_A), F32),
            jax.ShapeDtypeStruct((batch, 1, NH_A), F32),
        ],
        compiler_params=_params(("parallel", "arbitrary"), 32), name="mlstm_prompt",
    )(qt, k, vt, ot, gates_t, b_gates, g_norm)


MLSTM_DEC_TILE = 8


def _segments(n, width):
    return (_iota2((n, n // width), 0) // width == _iota2((n, n // width), 1)).astype(F32)


def _spread(x, seg):
    return lax.dot_general(x, seg, NT_DIMS, precision=lax.Precision.HIGHEST, preferred_element_type=F32)


def _mlstm_decode_kernel(q_ref, k_ref, v_ref, o_ref, gt_ref, bg_ref, gn_ref, c0_ref, n0_ref, m0_ref,
                         h_ref, c_ref, n_ref, m_ref, inter_sc):
    tb = q_ref.shape[0]
    ig, lf = _mlstm_gates(gt_ref[...], bg_ref[...])
    q, k = q_ref[...], k_ref[...] * (DQK_A ** -0.5)
    n_old = n0_ref[...]
    seg_qk, seg_v = _segments(NQK_A, DQK_A), _segments(NV_A, DV_A)
    a = lf + m0_ref[...]
    mt = jnp.maximum(a, ig)
    w_int = jnp.exp(a - mt)
    ws = jnp.exp(ig - mt)
    w = ws * _dot_exact(q * k, seg_qk)
    den = w_int * _dot_exact(q * n_old, seg_qk) + w
    k_w = _spread(ws, seg_qk) * k
    n_ref[...] = _spread(w_int, seg_qk) * n_old + k_w
    m_ref[...] = mt
    q16, kw16, v16 = q.astype(BF16), k_w.astype(BF16), v_ref[...].astype(BF16)
    seq = _iota2((tb, 1), 0)
    for j in range(tb):
        kw_j = jnp.where(seq == j, kw16, jnp.zeros_like(kw16))
        for h in range(NH_A):
            qs, vs = slice(h * DQK_A, (h + 1) * DQK_A), slice(h * DV_A, (h + 1) * DV_A)
            c_old = c0_ref[j, h]
            inter_sc[j:j + 1, vs] = _dot(q16[j:j + 1, qs], c_old.astype(BF16))
            outer = lax.dot_general(kw_j[:, qs], v16[:, vs], TN_DIMS, preferred_element_type=F32)
            c_ref[j, h] = w_int[j:j + 1, h:h + 1] * c_old + outer
    num = _spread(w_int, seg_v) * inter_sc[...] + _spread(w, seg_v) * v_ref[...]
    hc = num / _spread(jnp.maximum(jnp.abs(den), jnp.exp(-mt)), seg_v)
    inv_rms = lax.rsqrt(_dot_exact(hc * hc, seg_v) * (1.0 / DV_A) + EPS)
    h_ref[...] = hc * _spread(inv_rms, seg_v) * gn_ref[...] * _sigmoid(o_ref[...])


def _mlstm_decode(q, k, v, o, gates, b_gates, g_norm, c0, n0, m0):
    nb = q.shape[0]
    tb = MLSTM_DEC_TILE
    rows = lambda i: (i, 0)
    fixed = lambda i: (0, 0)
    return pl.pallas_call(
        _mlstm_decode_kernel,
        grid=(nb // tb,),
        in_specs=[
            pl.BlockSpec((tb, NQK_A), rows), pl.BlockSpec((tb, NQK_A), rows), pl.BlockSpec((tb, NV_A), rows),
            pl.BlockSpec((tb, NV_A), rows), pl.BlockSpec((tb, 2 * NH_A), rows),
            pl.BlockSpec((1, 2 * NH_A), fixed), pl.BlockSpec((1, NV_A), fixed),
            pl.BlockSpec((tb, NH_A, DQK_A, DV_A), lambda i: (i, 0, 0, 0)),
            pl.BlockSpec((tb, NQK_A), rows),
            pl.BlockSpec((tb, NH_A), rows),
        ],
        out_specs=[
            pl.BlockSpec((tb, NV_A), rows),
            pl.BlockSpec((tb, NH_A, DQK_A, DV_A), lambda i: (i, 0, 0, 0)),
            pl.BlockSpec((tb, NQK_A), rows),
            pl.BlockSpec((tb, NH_A), rows),
        ],
        out_shape=[
            jax.ShapeDtypeStruct((nb, NV_A), F32),
            jax.ShapeDtypeStruct(c0.shape, F32),
            jax.ShapeDtypeStruct(n0.shape, F32),
            jax.ShapeDtypeStruct(m0.shape, F32),
        ],
        scratch_shapes=[pltpu.VMEM((tb, NV_A), F32)],
        compiler_params=_params(("parallel",), 32), name="mlstm_decode",
    )(q, k, v, o, gates, b_gates, g_norm, c0, n0, m0)


def _fox_gates_kernel(fg_ref, bf_ref, lft_ref, qb_ref, kb_ref, d_sc):
    t = fg_ref.shape[0]
    lf = _log_sigmoid(fg_ref[...] + bf_ref[...])
    lft_ref[0] = lf.T
    tril = (_iota2((LANES, LANES), 0) >= _iota2((LANES, LANES), 1)).astype(F32)
    carry = jnp.zeros((1, N_HEADS), F32)
    for blk in range(t // LANES):
        rows = slice(blk * LANES, (blk + 1) * LANES)
        d = _dot_exact(tril, lf[rows]) + carry
        d_sc[rows, :] = d
        carry = d[LANES - 1:LANES, :]
    parts = _split3(d_sc[...] * LOG2E)
    head, lane = _iota2((N_HEADS, LANES), 0), _iota2((N_HEADS, LANES), 1)
    lane1 = _iota2((1, LANES), 1) % BIAS_LANES
    qb = jnp.where((lane1 >= 3) & (lane1 < 6), 1.0, 0.0)
    kb = jnp.where(lane1 < 3, 1.0, 0.0)
    for c, part in enumerate(parts):
        qb = qb + _dot(part, jnp.where(lane == head * BIAS_LANES + c, 1.0, 0.0).astype(BF16))
        kb = kb + _dot(part, jnp.where(lane == head * BIAS_LANES + 3 + c, -1.0, 0.0).astype(BF16))
    qb_ref[...] = qb.astype(BF16)
    kb_ref[...] = kb.astype(BF16)


def _fox_gates(fg, b_f, batch):
    m = fg.shape[0]
    t = m // batch
    return pl.pallas_call(
        _fox_gates_kernel,
        grid=(batch,),
        in_specs=[pl.BlockSpec((t, N_HEADS), lambda b: (b, 0)), pl.BlockSpec((1, N_HEADS), lambda b: (0, 0))],
        out_specs=[pl.BlockSpec((1, N_HEADS, t), lambda b: (b, 0, 0)),
                   pl.BlockSpec((t, LANES), lambda b: (b, 0)), pl.BlockSpec((t, LANES), lambda b: (b, 0))],
        out_shape=[jax.ShapeDtypeStruct((batch, N_HEADS, t), F32),
                   jax.ShapeDtypeStruct((m, LANES), BF16), jax.ShapeDtypeStruct((m, LANES), BF16)],
        scratch_shapes=[pltpu.VMEM((t, N_HEADS), F32)],
        compiler_params=_params(("parallel",), 32), name="fox_gates",
    )(fg, b_f)


def _head_lane_mask(hh):
    lane = _iota2((1, LANES), 1)
    return jnp.where((lane >= hh * D_HEAD) & (lane < (hh + 1) * D_HEAD), 1.0, 0.0).astype(BF16)


def _fox_prompt_kernel(q_ref, k_ref, vt_ref, qb_ref, kb_ref, o_ref, m_sc, acc_sc):
    qi, ki = pl.program_id(1), pl.program_id(2)
    tq, tk = q_ref.shape[0], k_ref.shape[0]
    n_pairs = N_HEADS // HEADS_PER_PAIR

    @pl.when(ki == 0)
    def _():
        m_sc[...] = jnp.full_like(m_sc, -jnp.inf)
        acc_sc[...] = jnp.zeros_like(acc_sc)

    def block(diagonal):
        qb, kb = qb_ref[...], kb_ref[...]
        ones = jnp.ones((D_HEAD, tk), BF16)
        bias_head = _iota2((1, LANES), 1) // BIAS_LANES
        visible = _iota2((tk, tq), 0) <= _iota2((tk, tq), 1)

        def pair(hp, carry):
            off = pl.multiple_of(hp * LANES, LANES)
            rhs = jnp.concatenate([q_ref[:, pl.ds(off, LANES)], qb], axis=1)
            kp = k_ref[:, pl.ds(off, LANES)]
            vtp = vt_ref[0, pl.ds(off, LANES), :]
            for hh in range(HEADS_PER_PAIR):
                head = hp * HEADS_PER_PAIR + hh
                kb_h = kb * jnp.where(bias_head == head, 1.0, 0.0).astype(BF16)
                s = _dot_nt(jnp.concatenate([kp * _head_lane_mask(hh), kb_h], axis=1), rhs)
                if diagonal:
                    s = jnp.where(visible, s, -jnp.inf)
                m_old = m_sc[head]
                m_new = jnp.maximum(m_old, jnp.max(s, axis=0, keepdims=True))
                p = jnp.exp2(s - m_new).astype(BF16)
                v_aug = (jnp.concatenate([vtp[:D_HEAD], ones], axis=0) if hh == 0
                         else jnp.concatenate([ones, vtp[D_HEAD:]], axis=0))
                acc_sc[head] = jnp.exp2(m_old - m_new) * acc_sc[head] + _dot(v_aug, p)
                m_sc[head] = m_new
            return carry

        lax.fori_loop(0, n_pairs, pair, 0, unroll=PAIR_UNROLL)

    @pl.when(ki < qi)
    def _():
        block(False)

    @pl.when(ki == qi)
    def _():
        block(True)

    @pl.when(ki == pl.num_programs(2) - 1)
    def _():
        def pair(hp, carry):
            off = pl.multiple_of(hp * LANES, LANES)
            a0, a1 = acc_sc[hp * HEADS_PER_PAIR], acc_sc[hp * HEADS_PER_PAIR + 1]
            o_t = jnp.concatenate([a0[:D_HEAD] * (1.0 / a0[D_HEAD:D_HEAD + 1]),
                                   a1[D_HEAD:] * (1.0 / a1[:1])], axis=0)
            o_ref[:, pl.ds(off, LANES)] = o_t.T.astype(o_ref.dtype)
            return carry

        lax.fori_loop(0, n_pairs, pair, 0)


def _fox_prompt(q, k, vt, qb, kb, batch):
    m = q.shape[0]
    t = m // batch
    tile = ATT_TILE
    n = t // tile
    qmap = lambda b, qi, ki: (b * n + qi, 0)
    kmap = lambda b, qi, ki: (b * n + jnp.minimum(ki, qi), 0)
    return pl.pallas_call(
        _fox_prompt_kernel,
        grid=(batch, n, n),
        in_specs=[
            pl.BlockSpec((tile, D_MODEL), qmap), pl.BlockSpec((tile, D_MODEL), kmap),
            pl.BlockSpec((1, D_MODEL, tile), lambda b, qi, ki: (b, 0, jnp.minimum(ki, qi))),
            pl.BlockSpec((tile, LANES), qmap), pl.BlockSpec((tile, LANES), kmap),
        ],
        out_specs=pl.BlockSpec((tile, D_MODEL), qmap),
        out_shape=jax.ShapeDtypeStruct((m, D_MODEL), BF16),
        scratch_shapes=[pltpu.VMEM((N_HEADS, 1, tile), F32), pltpu.VMEM((N_HEADS, LANES, tile), F32)],
        compiler_params=_params(("parallel", "parallel", "arbitrary"), 40), name="fox_prompt",
    )(q, k, vt, qb, kb)


SUFFIX_ROWS = SB_SUB + 16


def _suffix_lhs():
    r, c = _iota2((SUFFIX_ROWS, SB_SUB), 0), _iota2((SUFFIX_ROWS, SB_SUB), 1)
    u = jnp.where((c > r) | (r >= SB_SUB), 1.0, 0.0).astype(BF16)
    return jnp.concatenate([u, u], axis=1)


def _sb_prompt_kernel(q_ref, k_ref, vt_ref, o_ref, carry_sc, acc_sc):
    qi, kk = pl.program_id(1), pl.program_id(2)
    tq, tk = q_ref.shape[0], k_ref.shape[0]
    n_pairs = N_HEADS // HEADS_PER_PAIR
    n_sub = tk // SB_SUB

    @pl.when(kk == 0)
    def _():
        carry_sc[...] = jnp.zeros_like(carry_sc)
        acc_sc[...] = jnp.zeros_like(acc_sc)

    def block(diagonal):
        suffix_lhs = _suffix_lhs()
        zeros = jnp.zeros((D_HEAD, tk), BF16)
        visible = _iota2((tk, tq), 0) < _iota2((tk, tq), 1)

        def pair(hp, carry):
            off = pl.multiple_of(hp * LANES, LANES)
            qp = q_ref[:, pl.ds(off, LANES)]
            kp = k_ref[:, pl.ds(off, LANES)]
            vtp = vt_ref[0, pl.ds(off, LANES), :]
            for hh in range(HEADS_PER_PAIR):
                head = hp * HEADS_PER_PAIR + hh
                z = _dot_nt(kp * _head_lane_mask(hh), qp)
                nl = jnp.maximum(z, 0.0) + jnp.log2(1.0 + jnp.exp2(-jnp.abs(z)))
                if diagonal:
                    nl = jnp.where(visible, nl, 0.0)
                hi = nl.astype(BF16)
                lo = (nl - hi.astype(F32)).astype(BF16)
                newer = carry_sc[head]
                later = [None] * n_sub
                for sub in reversed(range(n_sub)):
                    keys = slice(sub * SB_SUB, (sub + 1) * SB_SUB)
                    sums = _dot(suffix_lhs, jnp.concatenate([hi[keys], lo[keys]], axis=0))
                    later[sub] = sums[:SB_SUB] + newer
                    newer = newer + sums[SB_SUB:SB_SUB + 1]
                carry_sc[head] = newer
                a = jnp.exp2(z - nl - jnp.concatenate(later, axis=0))
                if diagonal:
                    a = jnp.where(visible, a, 0.0)
                v_h = (jnp.concatenate([vtp[:D_HEAD], zeros], axis=0) if hh == 0
                       else jnp.concatenate([zeros, vtp[D_HEAD:]], axis=0))
                acc_sc[hp] += _dot(v_h, a.astype(BF16))
            return carry

        lax.fori_loop(0, n_pairs, pair, 0, unroll=PAIR_UNROLL)

    @pl.when(kk == 0)
    def _():
        block(True)

    @pl.when((kk > 0) & (kk <= qi))
    def _():
        block(False)

    @pl.when(kk == pl.num_programs(2) - 1)
    def _():
        def pair(hp, carry):
            off = pl.multiple_of(hp * LANES, LANES)
            o_ref[:, pl.ds(off, LANES)] = acc_sc[hp].T.astype(o_ref.dtype)
            return carry

        lax.fori_loop(0, n_pairs, pair, 0)


def _sb_prompt(q, k, vt, batch):
    m = q.shape[0]
    t = m // batch
    tile = ATT_TILE
    n = t // tile
    qmap = lambda b, qi, kk: (b * n + qi, 0)
    return pl.pallas_call(
        _sb_prompt_kernel,
        grid=(batch, n, n),
        in_specs=[
            pl.BlockSpec((tile, D_MODEL), qmap),
            pl.BlockSpec((tile, D_MODEL), lambda b, qi, kk: (b * n + jnp.maximum(qi - kk, 0), 0)),
            pl.BlockSpec((1, D_MODEL, tile), lambda b, qi, kk: (b, 0, jnp.maximum(qi - kk, 0))),
        ],
        out_specs=pl.BlockSpec((tile, D_MODEL), qmap),
        out_shape=jax.ShapeDtypeStruct((m, D_MODEL), BF16),
        scratch_shapes=[pltpu.VMEM((N_HEADS, 1, tile), F32),
                        pltpu.VMEM((N_HEADS // HEADS_PER_PAIR, LANES, tile), F32)],
        compiler_params=_params(("parallel", "parallel", "arbitrary"), 40), name="sb_prompt",
    )(q, k, vt)


def _page_spec(block, n_pages, slot):
    p = PAGES_PER_STEP
    return pl.BlockSpec(block, lambda b, j, pt: (pt[b, n_pages - 1 - (j * p + slot)],) + (0,) * (len(block) - 1))


def _load_query_columns(q_row, qcol_sc):
    for h in range(N_HEADS):
        qcol_sc[h] = jnp.broadcast_to(q_row[:, h * D_HEAD:(h + 1) * D_HEAD].T, (D_HEAD, PAGE_SIZE))


def _page_scores(k_refs, qcol_sc):
    q = qcol_sc[...]
    return jnp.concatenate([jnp.sum(k_ref[0] * q, axis=1) for k_ref in k_refs], axis=1)


def _page_suffix_sums(x, carry):
    p = x.shape[1] // PAGE_SIZE
    r, c = _iota2((PAGE_SIZE, 2 * PAGE_SIZE), 0), _iota2((PAGE_SIZE, 2 * PAGE_SIZE), 1)
    u = jnp.where((r > c) | (c >= PAGE_SIZE), 1.0, 0.0).astype(BF16)
    stacked = jnp.concatenate([x[:, i * PAGE_SIZE:(i + 1) * PAGE_SIZE] for i in range(p)], axis=0)
    sums = sum(_dot(part, u) for part in _split3(stacked))
    out = []
    for i in range(p):
        rows = slice(i * N_HEADS, (i + 1) * N_HEADS)
        out.append(sums[rows, :PAGE_SIZE] + carry)
        carry = carry + sums[rows, PAGE_SIZE:]
    return jnp.concatenate(out, axis=1), carry


def _accumulate_values(v_refs, w_sc, acc_sc, rescale):
    for h in range(N_HEADS):
        acc = acc_sc[h]
        if rescale is not None:
            acc = acc * rescale[h:h + 1, :]
        for i, v_ref in enumerate(v_refs):
            acc = acc + w_sc[h:h + 1, i * PAGE_SIZE:(i + 1) * PAGE_SIZE] * v_ref[0, h]
        acc_sc[h] = acc


def _store_head_sums(acc_sc, inv, o_ref):
    ones = jnp.ones((8, PAGE_SIZE), BF16)
    for h in range(N_HEADS):
        row = sum(_dot_nt(ones, part) for part in _split3(acc_sc[h]))[:1]
        if inv is not None:
            row = row * inv[h:h + 1, :]
        o_ref[0, :, h * D_HEAD:(h + 1) * D_HEAD] = row


def _fox_decode_kernel(pt_ref, q_ref, kn_ref, vn_ref, lfn_ref, *refs):
    p = PAGES_PER_STEP
    k_refs, v_refs, lf_refs = refs[:p], refs[p:2 * p], refs[2 * p:3 * p]
    o_ref, qcol_sc, w_sc, m_sc, l_sc, carry_sc, acc_sc = refs[3 * p:]
    j = pl.program_id(1)

    @pl.when(j == 0)
    def _():
        q = q_ref[0] * (D_HEAD ** -0.5)
        _load_query_columns(q, qcol_sc)
        seg = (_iota2((D_MODEL, N_HEADS), 0) // D_HEAD == _iota2((D_MODEL, N_HEADS), 1)).astype(F32)
        m_sc[...] = _dot_exact(q * kn_ref[0], seg).T
        l_sc[...] = jnp.ones_like(l_sc)
        lane0 = _iota2((D_HEAD, PAGE_SIZE), 1) == 0
        for h in range(N_HEADS):
            acc_sc[h] = jnp.where(lane0, vn_ref[0][:, h * D_HEAD:(h + 1) * D_HEAD].T, 0.0)
        carry_sc[...] = jnp.broadcast_to(lfn_ref[0].T, carry_sc.shape)

    logf = jnp.concatenate([lf_ref[0] for lf_ref in lf_refs], axis=1)
    bias, carry = _page_suffix_sums(logf, carry_sc[...])
    carry_sc[...] = carry
    s = _page_scores(k_refs, qcol_sc) + bias
    m_old = m_sc[...]
    m_new = jnp.maximum(m_old, jnp.max(s, axis=1, keepdims=True))
    alpha = jnp.exp(m_old - m_new)
    w = jnp.exp(s - m_new)
    w_sc[...] = w
    l_sc[...] = alpha * l_sc[...] + jnp.sum(w, axis=1, keepdims=True)
    m_sc[...] = m_new
    _accumulate_values(v_refs, w_sc, acc_sc, alpha)

    @pl.when(j == pl.num_programs(1) - 1)
    def _():
        _store_head_sums(acc_sc, 1.0 / l_sc[...], o_ref)


def _decode_scratch():
    p = PAGES_PER_STEP
    return [pltpu.VMEM((N_HEADS, D_HEAD, PAGE_SIZE), F32), pltpu.VMEM((N_HEADS, p * PAGE_SIZE), F32)]


def _fox_decode(q, k_new, v_new, lf_new, cache_k, cache_v, cache_lf, page_table):
    nb, n_pages = page_table.shape
    p = PAGES_PER_STEP
    row = lambda b, j, pt: (b, 0, 0)
    in_specs = [pl.BlockSpec((1, 1, D_MODEL), row)] * 3 + [pl.BlockSpec((1, 1, N_HEADS), row)]
    in_specs += [_page_spec((1, N_HEADS, D_HEAD, PAGE_SIZE), n_pages, i) for i in range(p)] * 2
    in_specs += [_page_spec((1, N_HEADS, PAGE_SIZE), n_pages, i) for i in range(p)]
    grid_spec = pltpu.PrefetchScalarGridSpec(
        num_scalar_prefetch=1, grid=(nb, n_pages // p), in_specs=in_specs,
        out_specs=pl.BlockSpec((1, 1, D_MODEL), row),
        scratch_shapes=_decode_scratch() + [
            pltpu.VMEM((N_HEADS, 1), F32), pltpu.VMEM((N_HEADS, 1), F32), pltpu.VMEM((N_HEADS, PAGE_SIZE), F32),
            pltpu.VMEM((N_HEADS, D_HEAD, PAGE_SIZE), F32)])
    return pl.pallas_call(
        _fox_decode_kernel, grid_spec=grid_spec,
        out_shape=jax.ShapeDtypeStruct((nb, 1, D_MODEL), F32),
        compiler_params=_params(("parallel", "arbitrary"), 40), name="fox_decode",
    )(page_table, q, k_new, v_new, lf_new, *([cache_k] * p), *([cache_v] * p), *([cache_lf] * p))


def _sb_decode_kernel(pt_ref, q_ref, *refs):
    p = PAGES_PER_STEP
    k_refs, v_refs = refs[:p], refs[p:2 * p]
    o_ref, qcol_sc, w_sc, carry_sc, acc_sc = refs[2 * p:]
    j = pl.program_id(1)

    @pl.when(j == 0)
    def _():
        _load_query_columns(q_ref[0] * (D_HEAD ** -0.5), qcol_sc)
        carry_sc[...] = jnp.zeros_like(carry_sc)
        acc_sc[...] = jnp.zeros_like(acc_sc)

    z = _page_scores(k_refs, qcol_sc)
    nl = _softplus(z)
    later, carry = _page_suffix_sums(nl, carry_sc[...])
    carry_sc[...] = carry
    w_sc[...] = jnp.exp(z - nl - later)
    _accumulate_values(v_refs, w_sc, acc_sc, None)

    @pl.when(j == pl.num_programs(1) - 1)
    def _():
        _store_head_sums(acc_sc, None, o_ref)


def _sb_decode(q, cache_k, cache_v, page_table):
    nb, n_pages = page_table.shape
    p = PAGES_PER_STEP
    row = lambda b, j, pt: (b, 0, 0)
    in_specs = [pl.BlockSpec((1, 1, D_MODEL), row)]
    in_specs += [_page_spec((1, N_HEADS, D_HEAD, PAGE_SIZE), n_pages, i) for i in range(p)] * 2
    grid_spec = pltpu.PrefetchScalarGridSpec(
        num_scalar_prefetch=1, grid=(nb, n_pages // p), in_specs=in_specs,
        out_specs=pl.BlockSpec((1, 1, D_MODEL), row),
        scratch_shapes=_decode_scratch() + [
            pltpu.VMEM((N_HEADS, PAGE_SIZE), F32), pltpu.VMEM((N_HEADS, D_HEAD, PAGE_SIZE), F32)])
    return pl.pallas_call(
        _sb_decode_kernel, grid_spec=grid_spec,
        out_shape=jax.ShapeDtypeStruct((nb, 1, D_MODEL), F32),
        compiler_params=_params(("parallel", "arbitrary"), 40), name="sb_decode",
    )(page_table, q, *([cache_k] * p), *([cache_v] * p))


def _fox_logf_kernel(fg_ref, bf_ref, lf_ref):
    lf_ref[...] = _log_sigmoid(fg_ref[...] + bf_ref[...])


def _fox_logf(fg, b_f):
    return pl.pallas_call(_fox_logf_kernel, out_shape=jax.ShapeDtypeStruct(fg.shape, F32), name="fox_logf")(fg, b_f)


def _heads_last(x_t, batch, seq):
    return x_t.reshape(batch, N_HEADS, D_HEAD, seq).transpose(0, 3, 1, 2)


def _pages_token_minor(cache):
    return cache.transpose(0, 2, 3, 1)


def kernel(x_prompt, x_sample, state_mlstm_c, state_mlstm_n, state_mlstm_m, cache_fox_k, cache_fox_v, cache_fox_logf, cache_sb_k, cache_sb_v, page_table, norm_mix, norm_ffn, norm_final, mlstm_w_in, mlstm_b_gates, mlstm_norm, mlstm_w_out, fox_w_in, fox_b_f, fox_w_out, sb_w_in, sb_w_out, ffn_w_gu, ffn_w_down):
    batch, seq, d = x_prompt.shape
    nb = x_sample.shape[0]
    xp = x_prompt.reshape(batch * seq, d)
    xs = x_sample.reshape(nb, d)
    q_scale = D_HEAD ** -0.5 * LOG2E
    out = {name: [] for name in ("pmc", "pmn", "pmm", "smc", "smn", "smm", "pfk", "pfv", "pfl", "sfk", "sfv", "sfl",
                                 "psk", "psv", "ssk", "ssv")}
    for i in range(DEPTH):
        j = i // N_MIXERS
        g_mix = norm_mix[i].reshape(1, d)
        if i % N_MIXERS == 0:
            w_in = mlstm_w_in[j].astype(BF16)
            splits = (NQK_A, 2 * NQK_A, 2 * NQK_A + NV_A, 2 * NQK_A + 2 * NV_A)
            ws = [w_in[:, a:b] for a, b in zip((0,) + splits, splits + (w_in.shape[1],))]
            b_gates = mlstm_b_gates[j].reshape(1, 2 * NH_A)
            g_norm = mlstm_norm[j].reshape(1, NV_A)
            w_out = mlstm_w_out[j].astype(BF16)
            plan = ((True, 1.0, (BF16,)), (False, 1.0, (BF16,)), (True, 1.0, (BF16,)), (True, 1.0, (F32,)),
                    (True, 1.0, (F32,)))
            qt, k, vt, ot, gt_t = _proj(xp, g_mix, [ws[0].T, ws[1], ws[2].T, ws[3].T, ws[4].T], plan, batch)
            ap, c1, n1, m1 = _mlstm_prompt(qt, k, vt, ot, gt_t, b_gates.reshape(2 * NH_A, 1),
                                           jnp.broadcast_to(g_norm.reshape(NV_A, 1), (NV_A, MLSTM_TILE)), batch)
            q, k, v, o, gt = _proj(xs, g_mix, ws, ((False, 1.0, (F32,)),) * 5, nb)
            a_s, c2, n2, m2 = _mlstm_decode(q, k, v, o, gt, b_gates, g_norm,
                                            state_mlstm_c[j], state_mlstm_n[j].reshape(nb, NQK_A), state_mlstm_m[j])
            out["pmc"].append(c1); out["pmn"].append(n1); out["pmm"].append(m1.reshape(batch, NH_A))
            out["smc"].append(c2); out["smn"].append(n2.reshape(nb, NH_A, DQK_A)); out["smm"].append(m2)
        elif i % N_MIXERS == 1:
            w_in = fox_w_in[j].astype(BF16)
            wq, wk, wv, wf = (w_in[:, a:b] for a, b in ((0, d), (d, 2 * d), (2 * d, 3 * d), (3 * d, 3 * d + N_HEADS)))
            b_f = fox_b_f[j].reshape(1, N_HEADS)
            w_out = fox_w_out[j].astype(BF16)
            plan = ((False, q_scale, (BF16,)), (False, 1.0, (BF16,)), (True, 1.0, (F32,)), (True, 1.0, (F32, BF16)),
                    (False, 1.0, (F32,)))
            q, kb16, k1t, v1t, vt16, fg = _proj(xp, g_mix, [wq, wk, wk.T, wv.T, wf], plan, batch)
            l1t, qbias, kbias = _fox_gates(fg, b_f, batch)
            ap = _fox_prompt(q, kb16, vt16, qbias, kbias, batch)
            q, k2, v2, fg = _proj(xs, g_mix, [wq, wk, wv, wf], ((False, 1.0, (F32,)),) * 4, nb)
            l2 = _fox_logf(fg, b_f)
            a_s = _fox_decode(q.reshape(nb, 1, d), k2.reshape(nb, 1, d), v2.reshape(nb, 1, d),
                              l2.reshape(nb, 1, N_HEADS),
                              _pages_token_minor(cache_fox_k[j]), _pages_token_minor(cache_fox_v[j]),
                              cache_fox_logf[j].transpose(0, 2, 1), page_table).reshape(nb, d)
            out["pfk"].append(_heads_last(k1t, batch, seq)); out["pfv"].append(_heads_last(v1t, batch, seq))
            out["pfl"].append(l1t.transpose(0, 2, 1))
            out["sfk"].append(k2.reshape(nb, 1, N_HEADS, D_HEAD)); out["sfv"].append(v2.reshape(nb, 1, N_HEADS, D_HEAD))
            out["sfl"].append(l2.reshape(nb, 1, N_HEADS))
        else:
            w_in = sb_w_in[j].astype(BF16)
            wq, wk, wv = (w_in[:, a:b] for a, b in ((0, d), (d, 2 * d), (2 * d, 3 * d)))
            w_out = sb_w_out[j].astype(BF16)
            plan = ((False, q_scale, (BF16,)), (False, 1.0, (BF16,)), (True, 1.0, (F32,)), (True, 1.0, (F32, BF16)))
            q, kb16, k1t, v1t, vt16 = _proj(xp, g_mix, [wq, wk, wk.T, wv.T], plan, batch)
            ap = _sb_prompt(q, kb16, vt16, batch)
            q, k2, v2 = _proj(xs, g_mix, [wq, wk, wv], ((False, 1.0, (F32,)),) * 3, nb)
            a_s = _sb_decode(q.reshape(nb, 1, d), _pages_token_minor(cache_sb_k[j]), _pages_token_minor(cache_sb_v[j]),
                             page_table).reshape(nb, d)
            out["psk"].append(_heads_last(k1t, batch, seq)); out["psv"].append(_heads_last(v1t, batch, seq))
            out["ssk"].append(k2.reshape(nb, 1, N_HEADS, D_HEAD)); out["ssv"].append(v2.reshape(nb, 1, N_HEADS, D_HEAD))
        g_ffn = norm_ffn[i].reshape(1, d)
        w_gu = ffn_w_gu[i].astype(BF16)
        w_down = ffn_w_down[i].astype(BF16)
        g_fin = norm_final.reshape(1, d) if i == DEPTH - 1 else None
        xp = _out_ffn(xp, ap, w_out, g_ffn, w_gu, w_down, g_fin)
        xs = _out_ffn(xs, a_s, w_out, g_ffn, w_gu, w_down, g_fin)
    y_prompt = xp.reshape(batch, seq, d)
    y_sample = xs.reshape(nb, 1, d)
    st = {name: jnp.stack(vals) for name, vals in out.items()}
    return (y_prompt, y_sample, st["pmc"], st["pmn"], st["pmm"], st["smc"], st["smn"], st["smm"],
            st["pfk"], st["pfv"], st["pfl"], st["sfk"], st["sfv"], st["sfl"],
            st["psk"], st["psv"], st["ssk"], st["ssv"])
```

```python
import functools

import jax
import jax.numpy as jnp
from jax import lax
from jax.experimental import pallas as pl
from jax.experimental.pallas import tpu as pltpu

F32 = jnp.float32
BF16 = jnp.bfloat16

D_MODEL = 1024
DEPTH = 4
N_MIXERS = 3
NH_A = 8
DV_A = D_MODEL // NH_A
DQK_A = DV_A // 2
NQK_A = NH_A * DQK_A
NV_A = NH_A * DV_A
MLSTM_CHUNK = 64
GATE_CAP = 15.0
N_HEADS = 16
D_HEAD = D_MODEL // N_HEADS
PAGE_SIZE = 128
D_FF = ((8 * D_MODEL // 3 + 255) // 256) * 256
EPS = 1e-6
LOG2E = 1.4426950408889634

LANES = 128
MIB = 1024 * 1024

ROW_TILE = 512
FFN_ROW_TILE = 512
FFN_COL_TILE = 1408
ATT_TILE = 512
PAIR_UNROLL = 2
SB_SUB = 128
PAGES_PER_STEP = 8
HEADS_PER_PAIR = LANES // D_HEAD
BIAS_LANES = LANES // N_HEADS

NT_DIMS = (((1,), (1,)), ((), ()))
TN_DIMS = (((0,), (0,)), ((), ()))


def _params(semantics, vmem_mib):
    return pltpu.CompilerParams(dimension_semantics=semantics, vmem_limit_bytes=vmem_mib * MIB)


def _dot(a, b):
    return jnp.dot(a, b, preferred_element_type=F32)


def _dot_nt(a, b):
    return lax.dot_general(a, b, NT_DIMS, preferred_element_type=F32)


def _dot_exact(a, b):
    return jnp.dot(a, b, precision=lax.Precision.HIGHEST, preferred_element_type=F32)


def _split3(x):
    hi = x.astype(BF16)
    r = x - hi.astype(F32)
    mid = r.astype(BF16)
    return hi, mid, (r - mid.astype(F32)).astype(BF16)


def _rms(x, g):
    return x * lax.rsqrt(jnp.mean(x * x, axis=-1, keepdims=True) + EPS) * g


def _softplus(x):
    return jnp.maximum(x, 0.0) + jnp.log1p(jnp.exp(-jnp.abs(x)))


def _log_sigmoid(x):
    return -_softplus(-x)


def _sigmoid(x):
    return 1.0 / (1.0 + jnp.exp(-x))


def _iota2(shape, dim):
    return lax.broadcasted_iota(jnp.int32, shape, dim)


def _proj_kernel(x_ref, g_ref, *refs, plan):
    n_w = len(plan)
    w_refs, o_refs = refs[:n_w], refs[n_w:]
    h = _rms(x_ref[...], g_ref[...]).astype(BF16)
    k = 0
    for w_ref, (transposed, scale, dts) in zip(w_refs, plan):
        y = _dot_nt(w_ref[...], h) if transposed else _dot(h, w_ref[...])
        if scale != 1.0:
            y = y * scale
        for dt in dts:
            if transposed:
                o_refs[k][0] = y.astype(dt)
            else:
                o_refs[k][...] = y.astype(dt)
            k += 1


def _proj(x, g, ws, plan, batch):
    m = x.shape[0]
    tm = min(m, ROW_TILE)
    tiles_per_seq = m // batch // tm if any(p[0] for p in plan) else 1
    in_specs = [pl.BlockSpec((tm, D_MODEL), lambda i: (i, 0)), pl.BlockSpec((1, D_MODEL), lambda i: (0, 0))]
    in_specs += [pl.BlockSpec(w.shape, lambda i: (0, 0)) for w in ws]
    out_shape, out_specs = [], []
    for w, (transposed, _, dts) in zip(ws, plan):
        for dt in dts:
            if transposed:
                n = w.shape[0]
                out_shape.append(jax.ShapeDtypeStruct((batch, n, m // batch), dt))
                out_specs.append(pl.BlockSpec((1, n, tm), lambda i: (i // tiles_per_seq, 0, i % tiles_per_seq)))
            else:
                n = w.shape[1]
                out_shape.append(jax.ShapeDtypeStruct((m, n), dt))
                out_specs.append(pl.BlockSpec((tm, n), lambda i: (i, 0)))
    return pl.pallas_call(
        functools.partial(_proj_kernel, plan=plan),
        grid=(m // tm,), in_specs=in_specs, out_specs=out_specs, out_shape=out_shape,
        compiler_params=_params(("parallel",), 56), name="proj",
    )(x, g, *ws)


def _out_ffn_kernel(x_ref, a_ref, wo_ref, g_ref, wg_ref, wu_ref, wd_ref, *refs, final_norm):
    gf_ref = refs[0] if final_norm else None
    o_ref, h_sc = refs[-2:]
    f = pl.program_id(1)
    last = pl.num_programs(1) - 1

    @pl.when(f == 0)
    def _():
        xm = x_ref[...] + _dot(a_ref[...].astype(BF16), wo_ref[...])
        o_ref[...] = xm
        h_sc[...] = _rms(xm, g_ref[...]).astype(BF16)

    h = h_sc[...]
    gate = _dot(h, wg_ref[...])
    up = _dot(h, wu_ref[...])
    act = (gate * _sigmoid(gate) * up).astype(BF16)
    total = o_ref[...] + _dot(act, wd_ref[...])
    if final_norm:
        @pl.when(f < last)
        def _():
            o_ref[...] = total

        @pl.when(f == last)
        def _():
            o_ref[...] = _rms(total, gf_ref[...])
    else:
        o_ref[...] = total


def _out_ffn(x, a, wo, g, wgu, wd, g_final=None):
    m = x.shape[0]
    tm = min(m, FFN_ROW_TILE)
    nf = D_FF // FFN_COL_TILE
    gain = pl.BlockSpec((1, D_MODEL), lambda i, f: (0, 0))
    final = () if g_final is None else (g_final,)
    return pl.pallas_call(
        functools.partial(_out_ffn_kernel, final_norm=g_final is not None),
        grid=(m // tm, nf),
        in_specs=[
            pl.BlockSpec((tm, D_MODEL), lambda i, f: (i, 0)),
            pl.BlockSpec((tm, D_MODEL), lambda i, f: (i, 0)),
            pl.BlockSpec((D_MODEL, D_MODEL), lambda i, f: (0, 0)),
            gain,
            pl.BlockSpec((D_MODEL, FFN_COL_TILE), lambda i, f: (0, f)),
            pl.BlockSpec((D_MODEL, FFN_COL_TILE), lambda i, f: (0, nf + f)),
            pl.BlockSpec((FFN_COL_TILE, D_MODEL), lambda i, f: (f, 0)),
        ] + [gain] * len(final),
        out_specs=pl.BlockSpec((tm, D_MODEL), lambda i, f: (i, 0)),
        out_shape=jax.ShapeDtypeStruct((m, D_MODEL), F32),
        scratch_shapes=[pltpu.VMEM((tm, D_MODEL), BF16)],
        compiler_params=_params(("parallel", "arbitrary"), 48), name="out_ffn",
    )(x, a, wo, g, wgu, wgu, wd, *final)


MLSTM_TILE = LANES


def _mlstm_gates(pre, bias):
    gates = GATE_CAP * jnp.tanh((pre + bias) / GATE_CAP)
    return gates[:, :NH_A], _log_sigmoid(gates[:, NH_A:])


def _mlstm_prompt_kernel(qt_ref, k_ref, vt_ref, ot_ref, gt_ref, bg_ref, gn_ref, h_ref, c_ref, n_ref, m_ref,
                         state_sc, m_sc):
    L = MLSTM_TILE
    aug = DV_A + 16

    @pl.when(pl.program_id(1) == 0)
    def _():
        state_sc[...] = jnp.zeros_like(state_sc)
        m_sc[...] = jnp.zeros_like(m_sc)

    gates = GATE_CAP * jnp.tanh((gt_ref[0] + bg_ref[...]) / GATE_CAP)
    ig, lf = gates[:NH_A], _log_sigmoid(gates[NH_A:])
    r_i, c_i = _iota2((L, 2 * L), 0), _iota2((L, 2 * L), 1)
    prefix = jnp.where((r_i <= c_i) | (c_i >= L), 1.0, 0.0).astype(BF16)
    sums = sum(_dot(part, prefix) for part in _split3(lf))
    bcum, btot = sums[:, :L], sums[:, L:]
    r = ig - bcum
    ones8, zeros16 = jnp.ones((NH_A, L), F32), jnp.zeros((2 * NH_A, L), F32)
    r_parts, b_parts = _split3(r), _split3(bcum)
    lhs_t = jnp.concatenate([p.astype(F32) for p in r_parts] + [ones8] * 3 + [zeros16], axis=0).T.astype(BF16)
    rhs_all = jnp.concatenate([ones8] * 3 + [p.astype(F32) for p in b_parts] + [zeros16], axis=0)
    rhs_head = _iota2((8 * NH_A, 1), 0) % NH_A
    causal = _iota2((L, L), 0) <= _iota2((L, L), 1)
    ones_rows = jnp.ones((aug - DV_A, L), BF16)
    scale = DQK_A ** -0.5
    for h in range(NH_A):
        hp, hh = divmod(h, HEADS_PER_PAIR)
        pair, vs = slice(hp * LANES, (hp + 1) * LANES), slice(h * DV_A, (h + 1) * DV_A)
        qt = qt_ref[0, pair, :]
        km = k_ref[:, pair] * _head_lane_mask(hh)
        v_aug = jnp.concatenate([vt_ref[0, vs, :], ones_rows], axis=0)
        state = state_sc[h]
        m_old = m_sc[h]
        dm = _dot(lhs_t, jnp.where(rhs_head == h, rhs_all, 0.0).astype(BF16))
        dm = jnp.where(causal, dm, -jnp.inf)
        a_row = bcum[h:h + 1] + m_old
        mt = jnp.maximum(a_row, jnp.max(dm, axis=0, keepdims=True))
        w_t = jnp.exp(dm - mt) * (_dot(km, qt) * scale)
        numden = jnp.exp(a_row - mt) * _dot(state.astype(BF16), qt) + _dot(v_aug, w_t.astype(BF16))
        hc = numden[:DV_A] / jnp.maximum(jnp.abs(numden[DV_A:DV_A + 1]), jnp.exp(-mt))
        hn = hc * lax.rsqrt(jnp.mean(hc * hc, axis=0, keepdims=True) + EPS)
        h_ref[:, vs] = (hn * gn_ref[vs, :] * _sigmoid(ot_ref[0, vs, :])).T.astype(h_ref.dtype)
        g_row = btot[h:h + 1] + r[h:h + 1]
        m_new = jnp.maximum(btot[h:h + 1] + m_old, jnp.max(g_row, axis=1, keepdims=True))
        decay = jnp.exp(btot[h:h + 1] + m_old - m_new)
        v_w = (v_aug.astype(F32) * (jnp.exp(g_row - m_new) * scale)).astype(BF16)
        state_sc[h] = decay * state + _dot(v_w, km)
        m_sc[h] = m_new

    @pl.when(pl.program_id(1) == pl.num_programs(1) - 1)
    def _():
        for h in range(NH_A):
            qk = slice((h % HEADS_PER_PAIR) * DQK_A, (h % HEADS_PER_PAIR + 1) * DQK_A)
            state = state_sc[h]
            c_ref[0, h] = state[:DV_A].T[qk]
            n_ref[0, h:h + 1, :] = state[DV_A:DV_A + 1, qk]
            m_ref[0, :, h:h + 1] = m_sc[h][:, :1]


def _mlstm_prompt(qt, k, vt, ot, gates_t, b_gates, g_norm, batch):
    m = k.shape[0]
    L = MLSTM_TILE
    nc = m // batch // L
    rows = lambda b, c: (b * nc + c, 0)
    cols = lambda b, c: (b, 0, c)
    fixed = lambda b, c: (0, 0)
    return pl.pallas_call(
        _mlstm_prompt_kernel,
        grid=(batch, nc),
        in_specs=[
            pl.BlockSpec((1, NQK_A, L), cols), pl.BlockSpec((L, NQK_A), rows), pl.BlockSpec((1, NV_A, L), cols),
            pl.BlockSpec((1, NV_A, L), cols), pl.BlockSpec((1, 2 * NH_A, L), cols),
            pl.BlockSpec((2 * NH_A, 1), fixed), pl.BlockSpec((NV_A, L), fixed),
        ],
        out_specs=[
            pl.BlockSpec((L, NV_A), rows),
            pl.BlockSpec((1, NH_A, DQK_A, DV_A), lambda b, c: (b, 0, 0, 0)),
            pl.BlockSpec((1, NH_A, DQK_A), lambda b, c: (b, 0, 0)),
            pl.BlockSpec((1, 1, NH_A), lambda b, c: (b, 0, 0)),
        ],
        scratch_shapes=[pltpu.VMEM((NH_A, DV_A + 16, LANES), F32), pltpu.VMEM((NH_A, 1, MLSTM_TILE), F32)],
        out_shape=[
            jax.ShapeDtypeStruct((m, NV_A), BF16),
            jax.ShapeDtypeStruct((batch, NH_A, DQK_A, DV_A), F32),
            jax.ShapeDtypeStruct((batch, NH_A, DQK_A), F32),
            jax.ShapeDtypeStruct((batch, 1, NH_A), F32),
        ],
        compiler_params=_params(("parallel", "arbitrary"), 32), name="mlstm_prompt",
    )(qt, k, vt, ot, gates_t, b_gates, g_norm)


MLSTM_DEC_TILE = 8


def _segments(n, width):
    return (_iota2((n, n // width), 0) // width == _iota2((n, n // width), 1)).astype(F32)


def _spread(x, seg):
    return lax.dot_general(x, seg, NT_DIMS, precision=lax.Precision.HIGHEST, preferred_element_type=F32)


def _mlstm_decode_kernel(q_ref, k_ref, v_ref, o_ref, gt_ref, bg_ref, gn_ref, c0_ref, n0_ref, m0_ref,
                         h_ref, c_ref, n_ref, m_ref, inter_sc):
    tb = q_ref.shape[0]
    ig, lf = _mlstm_gates(gt_ref[...], bg_ref[...])
    q, k = q_ref[...], k_ref[...] * (DQK_A ** -0.5)
    n_old = n0_ref[...]
    seg_qk, seg_v = _segments(NQK_A, DQK_A), _segments(NV_A, DV_A)
    a = lf + m0_ref[...]
    mt = jnp.maximum(a, ig)
    w_int = jnp.exp(a - mt)
    ws = jnp.exp(ig - mt)
    w = ws * _dot_exact(q * k, seg_qk)
    den = w_int * _dot_exact(q * n_old, seg_qk) + w
    k_w = _spread(ws, seg_qk) * k
    n_ref[...] = _spread(w_int, seg_qk) * n_old + k_w
    m_ref[...] = mt
    q16, kw16, v16 = q.astype(BF16), k_w.astype(BF16), v_ref[...].astype(BF16)
    seq = _iota2((tb, 1), 0)
    for j in range(tb):
        kw_j = jnp.where(seq == j, kw16, jnp.zeros_like(kw16))
        for h in range(NH_A):
            qs, vs = slice(h * DQK_A, (h + 1) * DQK_A), slice(h * DV_A, (h + 1) * DV_A)
            c_old = c0_ref[j, h]
            inter_sc[j:j + 1, vs] = _dot(q16[j:j + 1, qs], c_old.astype(BF16))
            outer = lax.dot_general(kw_j[:, qs], v16[:, vs], TN_DIMS, preferred_element_type=F32)
            c_ref[j, h] = w_int[j:j + 1, h:h + 1] * c_old + outer
    num = _spread(w_int, seg_v) * inter_sc[...] + _spread(w, seg_v) * v_ref[...]
    hc = num / _spread(jnp.maximum(jnp.abs(den), jnp.exp(-mt)), seg_v)
    inv_rms = lax.rsqrt(_dot_exact(hc * hc, seg_v) * (1.0 / DV_A) + EPS)
    h_ref[...] = hc * _spread(inv_rms, seg_v) * gn_ref[...] * _sigmoid(o_ref[...])


def _mlstm_decode(q, k, v, o, gates, b_gates, g_norm, c0, n0, m0):
    nb = q.shape[0]
    tb = MLSTM_DEC_TILE
    rows = lambda i: (i, 0)
    fixed = lambda i: (0, 0)
    return pl.pallas_call(
        _mlstm_decode_kernel,
        grid=(nb // tb,),
        in_specs=[
            pl.BlockSpec((tb, NQK_A), rows), pl.BlockSpec((tb, NQK_A), rows), pl.BlockSpec((tb, NV_A), rows),
            pl.BlockSpec((tb, NV_A), rows), pl.BlockSpec((tb, 2 * NH_A), rows),
            pl.BlockSpec((1, 2 * NH_A), fixed), pl.BlockSpec((1, NV_A), fixed),
            pl.BlockSpec((tb, NH_A, DQK_A, DV_A), lambda i: (i, 0, 0, 0)),
            pl.BlockSpec((tb, NQK_A), rows),
            pl.BlockSpec((tb, NH_A), rows),
        ],
        out_specs=[
            pl.BlockSpec((tb, NV_A), rows),
            pl.BlockSpec((tb, NH_A, DQK_A, DV_A), lambda i: (i, 0, 0, 0)),
            pl.BlockSpec((tb, NQK_A), rows),
            pl.BlockSpec((tb, NH_A), rows),
        ],
        out_shape=[
            jax.ShapeDtypeStruct((nb, NV_A), F32),
            jax.ShapeDtypeStruct(c0.shape, F32),
            jax.ShapeDtypeStruct(n0.shape, F32),
            jax.ShapeDtypeStruct(m0.shape, F32),
        ],
        scratch_shapes=[pltpu.VMEM((tb, NV_A), F32)],
        compiler_params=_params(("parallel",), 32), name="mlstm_decode",
    )(q, k, v, o, gates, b_gates, g_norm, c0, n0, m0)


def _fox_gates_kernel(fg_ref, bf_ref, lft_ref, qb_ref, kb_ref, d_sc):
    t = fg_ref.shape[0]
    lf = _log_sigmoid(fg_ref[...] + bf_ref[...])
    lft_ref[0] = lf.T
    tril = (_iota2((LANES, LANES), 0) >= _iota2((LANES, LANES), 1)).astype(F32)
    carry = jnp.zeros((1, N_HEADS), F32)
    for blk in range(t // LANES):
        rows = slice(blk * LANES, (blk + 1) * LANES)
        d = _dot_exact(tril, lf[rows]) + carry
        d_sc[rows, :] = d
        carry = d[LANES - 1:LANES, :]
    parts = _split3(d_sc[...] * LOG2E)
    head, lane = _iota2((N_HEADS, LANES), 0), _iota2((N_HEADS, LANES), 1)
    lane1 = _iota2((1, LANES), 1) % BIAS_LANES
    qb = jnp.where((lane1 >= 3) & (lane1 < 6), 1.0, 0.0)
    kb = jnp.where(lane1 < 3, 1.0, 0.0)
    for c, part in enumerate(parts):
        qb = qb + _dot(part, jnp.where(lane == head * BIAS_LANES + c, 1.0, 0.0).astype(BF16))
        kb = kb + _dot(part, jnp.where(lane == head * BIAS_LANES + 3 + c, -1.0, 0.0).astype(BF16))
    qb_ref[...] = qb.astype(BF16)
    kb_ref[...] = kb.astype(BF16)


def _fox_gates(fg, b_f, batch):
    m = fg.shape[0]
    t = m // batch
    return pl.pallas_call(
        _fox_gates_kernel,
        grid=(batch,),
        in_specs=[pl.BlockSpec((t, N_HEADS), lambda b: (b, 0)), pl.BlockSpec((1, N_HEADS), lambda b: (0, 0))],
        out_specs=[pl.BlockSpec((1, N_HEADS, t), lambda b: (b, 0, 0)),
                   pl.BlockSpec((t, LANES), lambda b: (b, 0)), pl.BlockSpec((t, LANES), lambda b: (b, 0))],
        out_shape=[jax.ShapeDtypeStruct((batch, N_HEADS, t), F32),
                   jax.ShapeDtypeStruct((m, LANES), BF16), jax.ShapeDtypeStruct((m, LANES), BF16)],
        scratch_shapes=[pltpu.VMEM((t, N_HEADS), F32)],
        compiler_params=_params(("parallel",), 32), name="fox_gates",
    )(fg, b_f)


def _head_lane_mask(hh):
    lane = _iota2((1, LANES), 1)
    return jnp.where((lane >= hh * D_HEAD) & (lane < (hh + 1) * D_HEAD), 1.0, 0.0).astype(BF16)


def _fox_prompt_kernel(q_ref, k_ref, vt_ref, qb_ref, kb_ref, o_ref, m_sc, acc_sc):
    qi, ki = pl.program_id(1), pl.program_id(2)
    tq, tk = q_ref.shape[0], k_ref.shape[0]
    n_pairs = N_HEADS // HEADS_PER_PAIR

    @pl.when(ki == 0)
    def _():
        m_sc[...] = jnp.full_like(m_sc, -jnp.inf)
        acc_sc[...] = jnp.zeros_like(acc_sc)

    def block(diagonal):
        qb, kb = qb_ref[...], kb_ref[...]
        ones = jnp.ones((D_HEAD, tk), BF16)
        bias_head = _iota2((1, LANES), 1) // BIAS_LANES
        visible = _iota2((tk, tq), 0) <= _iota2((tk, tq), 1)

        def pair(hp, carry):
            off = pl.multiple_of(hp * LANES, LANES)
            rhs = jnp.concatenate([q_ref[:, pl.ds(off, LANES)], qb], axis=1)
            kp = k_ref[:, pl.ds(off, LANES)]
            vtp = vt_ref[0, pl.ds(off, LANES), :]
            for hh in range(HEADS_PER_PAIR):
                head = hp * HEADS_PER_PAIR + hh
                kb_h = kb * jnp.where(bias_head == head, 1.0, 0.0).astype(BF16)
                s = _dot_nt(jnp.concatenate([kp * _head_lane_mask(hh), kb_h], axis=1), rhs)
                if diagonal:
                    s = jnp.where(visible, s, -jnp.inf)
                m_old = m_sc[head]
                m_new = jnp.maximum(m_old, jnp.max(s, axis=0, keepdims=True))
                p = jnp.exp2(s - m_new).astype(BF16)
                v_aug = (jnp.concatenate([vtp[:D_HEAD], ones], axis=0) if hh == 0
                         else jnp.concatenate([ones, vtp[D_HEAD:]], axis=0))
                acc_sc[head] = jnp.exp2(m_old - m_new) * acc_sc[head] + _dot(v_aug, p)
                m_sc[head] = m_new
            return carry

        lax.fori_loop(0, n_pairs, pair, 0, unroll=PAIR_UNROLL)

    @pl.when(ki < qi)
    def _():
        block(False)

    @pl.when(ki == qi)
    def _():
        block(True)

    @pl.when(ki == pl.num_programs(2) - 1)
    def _():
        def pair(hp, carry):
            off = pl.multiple_of(hp * LANES, LANES)
            a0, a1 = acc_sc[hp * HEADS_PER_PAIR], acc_sc[hp * HEADS_PER_PAIR + 1]
            o_t = jnp.concatenate([a0[:D_HEAD] * (1.0 / a0[D_HEAD:D_HEAD + 1]),
                                   a1[D_HEAD:] * (1.0 / a1[:1])], axis=0)
            o_ref[:, pl.ds(off, LANES)] = o_t.T.astype(o_ref.dtype)
            return carry

        lax.fori_loop(0, n_pairs, pair, 0)


def _fox_prompt(q, k, vt, qb, kb, batch):
    m = q.shape[0]
    t = m // batch
    tile = ATT_TILE
    n = t // tile
    qmap = lambda b, qi, ki: (b * n + qi, 0)
    kmap = lambda b, qi, ki: (b * n + jnp.minimum(ki, qi), 0)
    return pl.pallas_call(
        _fox_prompt_kernel,
        grid=(batch, n, n),
        in_specs=[
            pl.BlockSpec((tile, D_MODEL), qmap), pl.BlockSpec((tile, D_MODEL), kmap),
            pl.BlockSpec((1, D_MODEL, tile), lambda b, qi, ki: (b, 0, jnp.minimum(ki, qi))),
            pl.BlockSpec((tile, LANES), qmap), pl.BlockSpec((tile, LANES), kmap),
        ],
        out_specs=pl.BlockSpec((tile, D_MODEL), qmap),
        out_shape=jax.ShapeDtypeStruct((m, D_MODEL), BF16),
        scratch_shapes=[pltpu.VMEM((N_HEADS, 1, tile), F32), pltpu.VMEM((N_HEADS, LANES, tile), F32)],
        compiler_params=_params(("parallel", "parallel", "arbitrary"), 40), name="fox_prompt",
    )(q, k, vt, qb, kb)


SUFFIX_ROWS = SB_SUB + 16


def _suffix_lhs():
    r, c = _iota2((SUFFIX_ROWS, SB_SUB), 0), _iota2((SUFFIX_ROWS, SB_SUB), 1)
    u = jnp.where((c > r) | (r >= SB_SUB), 1.0, 0.0).astype(BF16)
    return jnp.concatenate([u, u], axis=1)


def _sb_prompt_kernel(q_ref, k_ref, vt_ref, o_ref, carry_sc, acc_sc):
    qi, kk = pl.program_id(1), pl.program_id(2)
    tq, tk = q_ref.shape[0], k_ref.shape[0]
    n_pairs = N_HEADS // HEADS_PER_PAIR
    n_sub = tk // SB_SUB

    @pl.when(kk == 0)
    def _():
        carry_sc[...] = jnp.zeros_like(carry_sc)
        acc_sc[...] = jnp.zeros_like(acc_sc)

    def block(diagonal):
        suffix_lhs = _suffix_lhs()
        zeros = jnp.zeros((D_HEAD, tk), BF16)
        visible = _iota2((tk, tq), 0) < _iota2((tk, tq), 1)

        def pair(hp, carry):
            off = pl.multiple_of(hp * LANES, LANES)
            qp = q_ref[:, pl.ds(off, LANES)]
            kp = k_ref[:, pl.ds(off, LANES)]
            vtp = vt_ref[0, pl.ds(off, LANES), :]
            for hh in range(HEADS_PER_PAIR):
                head = hp * HEADS_PER_PAIR + hh
                z = _dot_nt(kp * _head_lane_mask(hh), qp)
                nl = jnp.maximum(z, 0.0) + jnp.log2(1.0 + jnp.exp2(-jnp.abs(z)))
                if diagonal:
                    nl = jnp.where(visible, nl, 0.0)
                hi = nl.astype(BF16)
                lo = (nl - hi.astype(F32)).astype(BF16)
                newer = carry_sc[head]
                later = [None] * n_sub
                for sub in reversed(range(n_sub)):
                    keys = slice(sub * SB_SUB, (sub + 1) * SB_SUB)
                    sums = _dot(suffix_lhs, jnp.concatenate([hi[keys], lo[keys]], axis=0))
                    later[sub] = sums[:SB_SUB] + newer
                    newer = newer + sums[SB_SUB:SB_SUB + 1]
                carry_sc[head] = newer
                a = jnp.exp2(z - nl - jnp.concatenate(later, axis=0))
                if diagonal:
                    a = jnp.where(visible, a, 0.0)
                v_h = (jnp.concatenate([vtp[:D_HEAD], zeros], axis=0) if hh == 0
                       else jnp.concatenate([zeros, vtp[D_HEAD:]], axis=0))
                acc_sc[hp] += _dot(v_h, a.astype(BF16))
            return carry

        lax.fori_loop(0, n_pairs, pair, 0, unroll=PAIR_UNROLL)

    @pl.when(kk == 0)
    def _():
        block(True)

    @pl.when((kk > 0) & (kk <= qi))
    def _():
        block(False)

    @pl.when(kk == pl.num_programs(2) - 1)
    def _():
        def pair(hp, carry):
            off = pl.multiple_of(hp * LANES, LANES)
            o_ref[:, pl.ds(off, LANES)] = acc_sc[hp].T.astype(o_ref.dtype)
            return carry

        lax.fori_loop(0, n_pairs, pair, 0)


def _sb_prompt(q, k, vt, batch):
    m = q.shape[0]
    t = m // batch
    tile = ATT_TILE
    n = t // tile
    qmap = lambda b, qi, kk: (b * n + qi, 0)
    return pl.pallas_call(
        _sb_prompt_kernel,
        grid=(batch, n, n),
        in_specs=[
            pl.BlockSpec((tile, D_MODEL), qmap),
            pl.BlockSpec((tile, D_MODEL), lambda b, qi, kk: (b * n + jnp.maximum(qi - kk, 0), 0)),
            pl.BlockSpec((1, D_MODEL, tile), lambda b, qi, kk: (b, 0, jnp.maximum(qi - kk, 0))),
        ],
        out_specs=pl.BlockSpec((tile, D_MODEL), qmap),
        out_shape=jax.ShapeDtypeStruct((m, D_MODEL), BF16),
        scratch_shapes=[pltpu.VMEM((N_HEADS, 1, tile), F32),
                        pltpu.VMEM((N_HEADS // HEADS_PER_PAIR, LANES, tile), F32)],
        compiler_params=_params(("parallel", "parallel", "arbitrary"), 40), name="sb_prompt",
    )(q, k, vt)


def _page_spec(block, n_pages, slot):
    p = PAGES_PER_STEP
    return pl.BlockSpec(block, lambda b, j, pt: (pt[b, n_pages - 1 - (j * p + slot)],) + (0,) * (len(block) - 1))


def _block_diag_mask():
    return _iota2((N_HEADS, D_MODEL), 1) // D_HEAD == _iota2((N_HEADS, D_MODEL), 0)


def _page_matrix(page_ref):
    return page_ref[0].reshape(D_MODEL, PAGE_SIZE).astype(BF16)


def _page_scores(k_refs, qbd_sc):
    q = qbd_sc[...]
    return jnp.concatenate([_dot(q, _page_matrix(k_ref)) for k_ref in k_refs], axis=1)


def _page_suffix_sums(x, carry):
    p = x.shape[1] // PAGE_SIZE
    r, c = _iota2((PAGE_SIZE, 2 * PAGE_SIZE), 0), _iota2((PAGE_SIZE, 2 * PAGE_SIZE), 1)
    u = jnp.where((r > c) | (c >= PAGE_SIZE), 1.0, 0.0).astype(BF16)
    stacked = jnp.concatenate([x[:, i * PAGE_SIZE:(i + 1) * PAGE_SIZE] for i in range(p)], axis=0)
    sums = sum(_dot(part, u) for part in _split3(stacked))
    out = []
    for i in range(p):
        rows = slice(i * N_HEADS, (i + 1) * N_HEADS)
        out.append(sums[rows, :PAGE_SIZE] + carry)
        carry = carry + sums[rows, PAGE_SIZE:]
    return jnp.concatenate(out, axis=1), carry


def _accumulate_values(v_refs, w, acc_sc, rescale):
    acc = acc_sc[...]
    if rescale is not None:
        acc = acc * rescale
    for i, v_ref in enumerate(v_refs):
        acc = acc + _dot_nt(w[:, i * PAGE_SIZE:(i + 1) * PAGE_SIZE].astype(BF16), _page_matrix(v_ref))
    acc_sc[...] = acc


def _store_head_rows(acc, o_ref):
    o_ref[0] = jnp.sum(jnp.where(_block_diag_mask(), acc, 0.0), axis=0, keepdims=True)


def _fox_decode_kernel(pt_ref, q_ref, kn_ref, vn_ref, lfn_ref, *refs):
    p = PAGES_PER_STEP
    k_refs, v_refs, lf_refs = refs[:p], refs[p:2 * p], refs[2 * p:3 * p]
    o_ref, qbd_sc, m_sc, l_sc, carry_sc, acc_sc = refs[3 * p:]
    j = pl.program_id(1)

    @pl.when(j == 0)
    def _():
        qbd = jnp.where(_block_diag_mask(), q_ref[0] * (D_HEAD ** -0.5), 0.0)
        qbd_sc[...] = qbd.astype(BF16)
        m_sc[...] = jnp.sum(qbd * kn_ref[0], axis=1, keepdims=True)
        l_sc[...] = jnp.ones_like(l_sc)
        acc_sc[...] = jnp.broadcast_to(vn_ref[0], acc_sc.shape)
        carry_sc[...] = jnp.broadcast_to(lfn_ref[0].T, carry_sc.shape)

    logf = jnp.concatenate([lf_ref[0] for lf_ref in lf_refs], axis=1)
    bias, carry = _page_suffix_sums(logf, carry_sc[...])
    carry_sc[...] = carry
    s = _page_scores(k_refs, qbd_sc) + bias
    m_old = m_sc[...]
    m_new = jnp.maximum(m_old, jnp.max(s, axis=1, keepdims=True))
    alpha = jnp.exp(m_old - m_new)
    w = jnp.exp(s - m_new)
    l_sc[...] = alpha * l_sc[...] + jnp.sum(w, axis=1, keepdims=True)
    m_sc[...] = m_new
    _accumulate_values(v_refs, w, acc_sc, alpha)

    @pl.when(j == pl.num_programs(1) - 1)
    def _():
        _store_head_rows(acc_sc[...] * (1.0 / l_sc[...]), o_ref)


def _decode_scratch():
    return [pltpu.VMEM((N_HEADS, D_MODEL), BF16)]


def _fox_decode(q, k_new, v_new, lf_new, cache_k, cache_v, cache_lf, page_table):
    nb, n_pages = page_table.shape
    p = PAGES_PER_STEP
    row = lambda b, j, pt: (b, 0, 0)
    in_specs = [pl.BlockSpec((1, 1, D_MODEL), row)] * 3 + [pl.BlockSpec((1, 1, N_HEADS), row)]
    in_specs += [_page_spec((1, N_HEADS, D_HEAD, PAGE_SIZE), n_pages, i) for i in range(p)] * 2
    in_specs += [_page_spec((1, N_HEADS, PAGE_SIZE), n_pages, i) for i in range(p)]
    grid_spec = pltpu.PrefetchScalarGridSpec(
        num_scalar_prefetch=1, grid=(nb, n_pages // p), in_specs=in_specs,
        out_specs=pl.BlockSpec((1, 1, D_MODEL), row),
        scratch_shapes=_decode_scratch() + [
            pltpu.VMEM((N_HEADS, 1), F32), pltpu.VMEM((N_HEADS, 1), F32), pltpu.VMEM((N_HEADS, PAGE_SIZE), F32),
            pltpu.VMEM((N_HEADS, D_MODEL), F32)])
    return pl.pallas_call(
        _fox_decode_kernel, grid_spec=grid_spec,
        out_shape=jax.ShapeDtypeStruct((nb, 1, D_MODEL), F32),
        compiler_params=_params(("parallel", "arbitrary"), 40), name="fox_decode",
    )(page_table, q, k_new, v_new, lf_new, *([cache_k] * p), *([cache_v] * p), *([cache_lf] * p))


def _sb_decode_kernel(pt_ref, q_ref, *refs):
    p = PAGES_PER_STEP
    k_refs, v_refs = refs[:p], refs[p:2 * p]
    o_ref, qbd_sc, carry_sc, acc_sc = refs[2 * p:]
    j = pl.program_id(1)

    @pl.when(j == 0)
    def _():
        qbd_sc[...] = jnp.where(_block_diag_mask(), q_ref[0] * (D_HEAD ** -0.5), 0.0).astype(BF16)
        carry_sc[...] = jnp.zeros_like(carry_sc)
        acc_sc[...] = jnp.zeros_like(acc_sc)

    z = _page_scores(k_refs, qbd_sc)
    nl = _softplus(z)
    later, carry = _page_suffix_sums(nl, carry_sc[...])
    carry_sc[...] = carry
    _accumulate_values(v_refs, jnp.exp(z - nl - later), acc_sc, None)

    @pl.when(j == pl.num_programs(1) - 1)
    def _():
        _store_head_rows(acc_sc[...], o_ref)


def _sb_decode(q, cache_k, cache_v, page_table):
    nb, n_pages = page_table.shape
    p = PAGES_PER_STEP
    row = lambda b, j, pt: (b, 0, 0)
    in_specs = [pl.BlockSpec((1, 1, D_MODEL), row)]
    in_specs += [_page_spec((1, N_HEADS, D_HEAD, PAGE_SIZE), n_pages, i) for i in range(p)] * 2
    grid_spec = pltpu.PrefetchScalarGridSpec(
        num_scalar_prefetch=1, grid=(nb, n_pages // p), in_specs=in_specs,
        out_specs=pl.BlockSpec((1, 1, D_MODEL), row),
        scratch_shapes=_decode_scratch() + [
            pltpu.VMEM((N_HEADS, PAGE_SIZE), F32), pltpu.VMEM((N_HEADS, D_MODEL), F32)])
    return pl.pallas_call(
        _sb_decode_kernel, grid_spec=grid_spec,
        out_shape=jax.ShapeDtypeStruct((nb, 1, D_MODEL), F32),
        compiler_params=_params(("parallel", "arbitrary"), 40), name="sb_decode",
    )(page_table, q, *([cache_k] * p), *([cache_v] * p))


def _fox_logf_kernel(fg_ref, bf_ref, lf_ref):
    lf_ref[...] = _log_sigmoid(fg_ref[...] + bf_ref[...])


def _fox_logf(fg, b_f):
    return pl.pallas_call(_fox_logf_kernel, out_shape=jax.ShapeDtypeStruct(fg.shape, F32), name="fox_logf")(fg, b_f)


def _heads_last(x_t, batch, seq):
    return x_t.reshape(batch, N_HEADS, D_HEAD, seq).transpose(0, 3, 1, 2)


def _pages_token_minor(cache):
    return cache.transpose(0, 2, 3, 1)


def kernel(x_prompt, x_sample, state_mlstm_c, state_mlstm_n, state_mlstm_m, cache_fox_k, cache_fox_v, cache_fox_logf, cache_sb_k, cache_sb_v, page_table, norm_mix, norm_ffn, norm_final, mlstm_w_in, mlstm_b_gates, mlstm_norm, mlstm_w_out, fox_w_in, fox_b_f, fox_w_out, sb_w_in, sb_w_out, ffn_w_gu, ffn_w_down):
    batch, seq, d = x_prompt.shape
    nb = x_sample.shape[0]
    xp = x_prompt.reshape(batch * seq, d)
    xs = x_sample.reshape(nb, d)
    q_scale = D_HEAD ** -0.5 * LOG2E
    out = {name: [] for name in ("pmc", "pmn", "pmm", "smc", "smn", "smm", "pfk", "pfv", "pfl", "sfk", "sfv", "sfl",
                                 "psk", "psv", "ssk", "ssv")}
    for i in range(DEPTH):
        j = i // N_MIXERS
        g_mix = norm_mix[i].reshape(1, d)
        if i % N_MIXERS == 0:
            w_in = mlstm_w_in[j].astype(BF16)
            splits = (NQK_A, 2 * NQK_A, 2 * NQK_A + NV_A, 2 * NQK_A + 2 * NV_A)
            ws = [w_in[:, a:b] for a, b in zip((0,) + splits, splits + (w_in.shape[1],))]
            b_gates = mlstm_b_gates[j].reshape(1, 2 * NH_A)
            g_norm = mlstm_norm[j].reshape(1, NV_A)
            w_out = mlstm_w_out[j].astype(BF16)
            plan = ((True, 1.0, (BF16,)), (False, 1.0, (BF16,)), (True, 1.0, (BF16,)), (True, 1.0, (F32,)),
                    (True, 1.0, (F32,)))
            qt, k, vt, ot, gt_t = _proj(xp, g_mix, [ws[0].T, ws[1], ws[2].T, ws[3].T, ws[4].T], plan, batch)
            ap, c1, n1, m1 = _mlstm_prompt(qt, k, vt, ot, gt_t, b_gates.reshape(2 * NH_A, 1),
                                           jnp.broadcast_to(g_norm.reshape(NV_A, 1), (NV_A, MLSTM_TILE)), batch)
            q, k, v, o, gt = _proj(xs, g_mix, ws, ((False, 1.0, (F32,)),) * 5, nb)
            a_s, c2, n2, m2 = _mlstm_decode(q, k, v, o, gt, b_gates, g_norm,
                                            state_mlstm_c[j], state_mlstm_n[j].reshape(nb, NQK_A), state_mlstm_m[j])
            out["pmc"].append(c1); out["pmn"].append(n1); out["pmm"].append(m1.reshape(batch, NH_A))
            out["smc"].append(c2); out["smn"].append(n2.reshape(nb, NH_A, DQK_A)); out["smm"].append(m2)
        elif i % N_MIXERS == 1:
            w_in = fox_w_in[j].astype(BF16)
            wq, wk, wv, wf = (w_in[:, a:b] for a, b in ((0, d), (d, 2 * d), (2 * d, 3 * d), (3 * d, 3 * d + N_HEADS)))
            b_f = fox_b_f[j].reshape(1, N_HEADS)
            w_out = fox_w_out[j].astype(BF16)
            plan = ((False, q_scale, (BF16,)), (False, 1.0, (BF16,)), (True, 1.0, (F32,)), (True, 1.0, (F32, BF16)),
                    (False, 1.0, (F32,)))
            q, kb16, k1t, v1t, vt16, fg = _proj(xp, g_mix, [wq, wk, wk.T, wv.T, wf], plan, batch)
            l1t, qbias, kbias = _fox_gates(fg, b_f, batch)
            ap = _fox_prompt(q, kb16, vt16, qbias, kbias, batch)
            q, k2, v2, fg = _proj(xs, g_mix, [wq, wk, wv, wf], ((False, 1.0, (F32,)),) * 4, nb)
            l2 = _fox_logf(fg, b_f)
            a_s = _fox_decode(q.reshape(nb, 1, d), k2.reshape(nb, 1, d), v2.reshape(nb, 1, d),
                              l2.reshape(nb, 1, N_HEADS),
                              _pages_token_minor(cache_fox_k[j]), _pages_token_minor(cache_fox_v[j]),
                              cache_fox_logf[j].transpose(0, 2, 1), page_table).reshape(nb, d)
            out["pfk"].append(_heads_last(k1t, batch, seq)); out["pfv"].append(_heads_last(v1t, batch, seq))
            out["pfl"].append(l1t.transpose(0, 2, 1))
            out["sfk"].append(k2.reshape(nb, 1, N_HEADS, D_HEAD)); out["sfv"].append(v2.reshape(nb, 1, N_HEADS, D_HEAD))
            out["sfl"].append(l2.reshape(nb, 1, N_HEADS))
        else:
            w_in = sb_w_in[j].astype(BF16)
            wq, wk, wv = (w_in[:, a:b] for a, b in ((0, d), (d, 2 * d), (2 * d, 3 * d)))
            w_out = sb_w_out[j].astype(BF16)
            plan = ((False, q_scale, (BF16,)), (False, 1.0, (BF16,)), (True, 1.0, (F32,)), (True, 1.0, (F32, BF16)))
            q, kb16, k1t, v1t, vt16 = _proj(xp, g_mix, [wq, wk, wk.T, wv.T], plan, batch)
            ap = _sb_prompt(q, kb16, vt16, batch)
            q, k2, v2 = _proj(xs, g_mix, [wq, wk, wv], ((False, 1.0, (F32,)),) * 3, nb)
            a_s = _sb_decode(q.reshape(nb, 1, d), _pages_token_minor(cache_sb_k[j]), _pages_token_minor(cache_sb_v[j]),
                             page_table).reshape(nb, d)
            out["psk"].append(_heads_last(k1t, batch, seq)); out["psv"].append(_heads_last(v1t, batch, seq))
            out["ssk"].append(k2.reshape(nb, 1, N_HEADS, D_HEAD)); out["ssv"].append(v2.reshape(nb, 1, N_HEADS, D_HEAD))
        g_ffn = norm_ffn[i].reshape(1, d)
        w_gu = ffn_w_gu[i].astype(BF16)
        w_down = ffn_w_down[i].astype(BF16)
        g_fin = norm_final.reshape(1, d) if i == DEPTH - 1 else None
        xp = _out_ffn(xp, ap, w_out, g_ffn, w_gu, w_down, g_fin)
        xs = _out_ffn(xs, a_s, w_out, g_ffn, w_gu, w_down, g_fin)
    y_prompt = xp.reshape(batch, seq, d)
    y_sample = xs.reshape(nb, 1, d)
    st = {name: jnp.stack(vals) for name, vals in out.items()}
    return (y_prompt, y_sample, st["pmc"], st["pmn"], st["pmm"], st["smc"], st["smn"], st["smm"],
            st["pfk"], st["pfv"], st["pfl"], st["sfk"], st["sfv"], st["sfl"],
            st["psk"], st["psv"], st["ssk"], st["ssv"])
```

```python
import functools

import jax
import jax.numpy as jnp
from jax import lax
from jax.experimental import pallas as pl
from jax.experimental.pallas import tpu as pltpu

F32 = jnp.float32
BF16 = jnp.bfloat16

D_MODEL = 1024
DEPTH = 4
N_MIXERS = 3
NH_A = 8
DV_A = D_MODEL // NH_A
DQK_A = DV_A // 2
NQK_A = NH_A * DQK_A
NV_A = NH_A * DV_A
MLSTM_CHUNK = 64
GATE_CAP = 15.0
N_HEADS = 16
D_HEAD = D_MODEL // N_HEADS
PAGE_SIZE = 128
D_FF = ((8 * D_MODEL // 3 + 255) // 256) * 256
EPS = 1e-6
LOG2E = 1.4426950408889634

LANES = 128
MIB = 1024 * 1024

ROW_TILE = 512
FFN_ROW_TILE = 512
FFN_COL_TILE = 1408
ATT_TILE = 512
PAIR_UNROLL = 4
SB_SUB = 128
PAGES_PER_STEP = 8
HEADS_PER_PAIR = LANES // D_HEAD
BIAS_LANES = LANES // N_HEADS

NT_DIMS = (((1,), (1,)), ((), ()))
TN_DIMS = (((0,), (0,)), ((), ()))


def _params(semantics, vmem_mib):
    return pltpu.CompilerParams(dimension_semantics=semantics, vmem_limit_bytes=vmem_mib * MIB)


def _dot(a, b):
    return jnp.dot(a, b, preferred_element_type=F32)


def _dot_nt(a, b):
    return lax.dot_general(a, b, NT_DIMS, preferred_element_type=F32)


def _dot_exact(a, b):
    return jnp.dot(a, b, precision=lax.Precision.HIGHEST, preferred_element_type=F32)


def _split3(x):
    hi = x.astype(BF16)
    r = x - hi.astype(F32)
    mid = r.astype(BF16)
    return hi, mid, (r - mid.astype(F32)).astype(BF16)


def _rms(x, g):
    return x * lax.rsqrt(jnp.mean(x * x, axis=-1, keepdims=True) + EPS) * g


def _softplus(x):
    return jnp.maximum(x, 0.0) + jnp.log1p(jnp.exp(-jnp.abs(x)))


def _log_sigmoid(x):
    return -_softplus(-x)


def _sigmoid(x):
    return 1.0 / (1.0 + jnp.exp(-x))


def _iota2(shape, dim):
    return lax.broadcasted_iota(jnp.int32, shape, dim)


def _proj_kernel(x_ref, g_ref, *refs, plan):
    n_w = len(plan)
    w_refs, o_refs = refs[:n_w], refs[n_w:]
    h = _rms(x_ref[...], g_ref[...]).astype(BF16)
    k = 0
    for w_ref, (transposed, scale, dts) in zip(w_refs, plan):
        y = _dot_nt(w_ref[...], h) if transposed else _dot(h, w_ref[...])
        if scale != 1.0:
            y = y * scale
        for dt in dts:
            if transposed:
                o_refs[k][0] = y.astype(dt)
            else:
                o_refs[k][...] = y.astype(dt)
            k += 1


def _proj(x, g, ws, plan, batch):
    m = x.shape[0]
    tm = min(m, ROW_TILE)
    tiles_per_seq = m // batch // tm if any(p[0] for p in plan) else 1
    in_specs = [pl.BlockSpec((tm, D_MODEL), lambda i: (i, 0)), pl.BlockSpec((1, D_MODEL), lambda i: (0, 0))]
    in_specs += [pl.BlockSpec(w.shape, lambda i: (0, 0)) for w in ws]
    out_shape, out_specs = [], []
    for w, (transposed, _, dts) in zip(ws, plan):
        for dt in dts:
            if transposed:
                n = w.shape[0]
                out_shape.append(jax.ShapeDtypeStruct((batch, n, m // batch), dt))
                out_specs.append(pl.BlockSpec((1, n, tm), lambda i: (i // tiles_per_seq, 0, i % tiles_per_seq)))
            else:
                n = w.shape[1]
                out_shape.append(jax.ShapeDtypeStruct((m, n), dt))
                out_specs.append(pl.BlockSpec((tm, n), lambda i: (i, 0)))
    return pl.pallas_call(
        functools.partial(_proj_kernel, plan=plan),
        grid=(m // tm,), in_specs=in_specs, out_specs=out_specs, out_shape=out_shape,
        compiler_params=_params(("parallel",), 56), name="proj",
    )(x, g, *ws)


def _out_ffn_kernel(x_ref, a_ref, wo_ref, g_ref, wg_ref, wu_ref, wd_ref, *refs, final_norm):
    gf_ref = refs[0] if final_norm else None
    o_ref, h_sc = refs[-2:]
    f = pl.program_id(1)
    last = pl.num_programs(1) - 1

    @pl.when(f == 0)
    def _():
        xm = x_ref[...] + _dot(a_ref[...].astype(BF16), wo_ref[...])
        o_ref[...] = xm
        h_sc[...] = _rms(xm, g_ref[...]).astype(BF16)

    h = h_sc[...]
    gate = _dot(h, wg_ref[...])
    up = _dot(h, wu_ref[...])
    act = (gate * _sigmoid(gate) * up).astype(BF16)
    total = o_ref[...] + _dot(act, wd_ref[...])
    if final_norm:
        @pl.when(f < last)
        def _():
            o_ref[...] = total

        @pl.when(f == last)
        def _():
            o_ref[...] = _rms(total, gf_ref[...])
    else:
        o_ref[...] = total


def _out_ffn(x, a, wo, g, wgu, wd, layer, g_final=None):
    m = x.shape[0]
    tm = min(m, FFN_ROW_TILE)
    nf = D_FF // FFN_COL_TILE
    gain = pl.BlockSpec((1, D_MODEL), lambda i, f: (0, 0))
    final = () if g_final is None else (g_final,)
    return pl.pallas_call(
        functools.partial(_out_ffn_kernel, final_norm=g_final is not None),
        grid=(m // tm, nf),
        in_specs=[
            pl.BlockSpec((tm, D_MODEL), lambda i, f: (i, 0)),
            pl.BlockSpec((tm, D_MODEL), lambda i, f: (i, 0)),
            pl.BlockSpec((D_MODEL, D_MODEL), lambda i, f: (0, 0)),
            gain,
            pl.BlockSpec((None, D_MODEL, FFN_COL_TILE), lambda i, f: (layer, 0, f)),
            pl.BlockSpec((None, D_MODEL, FFN_COL_TILE), lambda i, f: (layer, 0, nf + f)),
            pl.BlockSpec((None, FFN_COL_TILE, D_MODEL), lambda i, f: (layer, f, 0)),
        ] + [gain] * len(final),
        out_specs=pl.BlockSpec((tm, D_MODEL), lambda i, f: (i, 0)),
        out_shape=jax.ShapeDtypeStruct((m, D_MODEL), F32),
        scratch_shapes=[pltpu.VMEM((tm, D_MODEL), BF16)],
        compiler_params=_params(("parallel", "arbitrary"), 48), name="out_ffn",
    )(x, a, wo, g, wgu, wgu, wd, *final)


MLSTM_TILE = LANES


def _mlstm_gates(pre, bias):
    gates = GATE_CAP * jnp.tanh((pre + bias) / GATE_CAP)
    return gates[:, :NH_A], _log_sigmoid(gates[:, NH_A:])


def _mlstm_prompt_kernel(qt_ref, k_ref, vt_ref, ot_ref, gt_ref, bg_ref, gn_ref, h_ref, c_ref, n_ref, m_ref,
                         state_sc, m_sc):
    L = MLSTM_TILE
    aug = DV_A + 16

    @pl.when(pl.program_id(1) == 0)
    def _():
        state_sc[...] = jnp.zeros_like(state_sc)
        m_sc[...] = jnp.zeros_like(m_sc)

    gates = GATE_CAP * jnp.tanh((gt_ref[0] + bg_ref[...]) / GATE_CAP)
    ig, lf = gates[:NH_A], _log_sigmoid(gates[NH_A:])
    r_i, c_i = _iota2((L, 2 * L), 0), _iota2((L, 2 * L), 1)
    prefix = jnp.where((r_i <= c_i) | (c_i >= L), 1.0, 0.0).astype(BF16)
    sums = sum(_dot(part, prefix) for part in _split3(lf))
    bcum, btot = sums[:, :L], sums[:, L:]
    r = ig - bcum
    ones8, zeros16 = jnp.ones((NH_A, L), F32), jnp.zeros((2 * NH_A, L), F32)
    r_parts, b_parts = _split3(r), _split3(bcum)
    lhs_t = jnp.concatenate([p.astype(F32) for p in r_parts] + [ones8] * 3 + [zeros16], axis=0).T.astype(BF16)
    rhs_all = jnp.concatenate([ones8] * 3 + [p.astype(F32) for p in b_parts] + [zeros16], axis=0)
    rhs_head = _iota2((8 * NH_A, 1), 0) % NH_A
    causal = _iota2((L, L), 0) <= _iota2((L, L), 1)
    ones_rows = jnp.ones((aug - DV_A, L), BF16)
    scale = DQK_A ** -0.5
    for h in range(NH_A):
        hp, hh = divmod(h, HEADS_PER_PAIR)
        pair, vs = slice(hp * LANES, (hp + 1) * LANES), slice(h * DV_A, (h + 1) * DV_A)
        qt = qt_ref[0, pair, :]
        km = k_ref[:, pair] * _head_lane_mask(hh)
        v_aug = jnp.concatenate([vt_ref[0, vs, :], ones_rows], axis=0)
        state = state_sc[h]
        m_old = m_sc[h]
        dm = _dot(lhs_t, jnp.where(rhs_head == h, rhs_all, 0.0).astype(BF16))
        dm = jnp.where(causal, dm, -jnp.inf)
        a_row = bcum[h:h + 1] + m_old
        mt = jnp.maximum(a_row, jnp.max(dm, axis=0, keepdims=True))
        w_t = jnp.exp(dm - mt) * (_dot(km, qt) * scale)
        numden = jnp.exp(a_row - mt) * _dot(state.astype(BF16), qt) + _dot(v_aug, w_t.astype(BF16))
        hc = numden[:DV_A] / jnp.maximum(jnp.abs(numden[DV_A:DV_A + 1]), jnp.exp(-mt))
        hn = hc * lax.rsqrt(jnp.mean(hc * hc, axis=0, keepdims=True) + EPS)
        h_ref[:, vs] = (hn * gn_ref[vs, :] * _sigmoid(ot_ref[0, vs, :])).T.astype(h_ref.dtype)
        g_row = btot[h:h + 1] + r[h:h + 1]
        m_new = jnp.maximum(btot[h:h + 1] + m_old, jnp.max(g_row, axis=1, keepdims=True))
        decay = jnp.exp(btot[h:h + 1] + m_old - m_new)
        v_w = (v_aug.astype(F32) * (jnp.exp(g_row - m_new) * scale)).astype(BF16)
        state_sc[h] = decay * state + _dot(v_w, km)
        m_sc[h] = m_new

    @pl.when(pl.program_id(1) == pl.num_programs(1) - 1)
    def _():
        for h in range(NH_A):
            qk = slice((h % HEADS_PER_PAIR) * DQK_A, (h % HEADS_PER_PAIR + 1) * DQK_A)
            state = state_sc[h]
            c_ref[0, h] = state[:DV_A].T[qk]
            n_ref[0, h:h + 1, :] = state[DV_A:DV_A + 1, qk]
            m_ref[0, :, h:h + 1] = m_sc[h][:, :1]


def _mlstm_prompt(qt, k, vt, ot, gates_t, b_gates, g_norm, batch):
    m = k.shape[0]
    L = MLSTM_TILE
    nc = m // batch // L
    rows = lambda b, c: (b * nc + c, 0)
    cols = lambda b, c: (b, 0, c)
    fixed = lambda b, c: (0, 0)
    return pl.pallas_call(
        _mlstm_prompt_kernel,
        grid=(batch, nc),
        in_specs=[
            pl.BlockSpec((1, NQK_A, L), cols), pl.BlockSpec((L, NQK_A), rows), pl.BlockSpec((1, NV_A, L), cols),
            pl.BlockSpec((1, NV_A, L), cols), pl.BlockSpec((1, 2 * NH_A, L), cols),
            pl.BlockSpec((2 * NH_A, 1), fixed), pl.BlockSpec((NV_A, L), fixed),
        ],
        out_specs=[
            pl.BlockSpec((L, NV_A), rows),
            pl.BlockSpec((1, NH_A, DQK_A, DV_A), lambda b, c: (b, 0, 0, 0)),
            pl.BlockSpec((1, NH_A, DQK_A), lambda b, c: (b, 0, 0)),
            pl.BlockSpec((1, 1, NH_A), lambda b, c: (b, 0, 0)),
        ],
        scratch_shapes=[pltpu.VMEM((NH_A, DV_A + 16, LANES), F32), pltpu.VMEM((NH_A, 1, MLSTM_TILE), F32)],
        out_shape=[
            jax.ShapeDtypeStruct((m, NV_A), BF16),
            jax.ShapeDtypeStruct((batch, NH_A, DQK_A, DV_A), F32),
            jax.ShapeDtypeStruct((batch, NH_A, DQK_A), F32),
            jax.ShapeDtypeStruct((batch, 1, NH_A), F32),
        ],
        compiler_params=_params(("parallel", "arbitrary"), 32), name="mlstm_prompt",
    )(qt, k, vt, ot, gates_t, b_gates, g_norm)


MLSTM_DEC_TILE = 8


def _segments(n, width):
    return (_iota2((n, n // width), 0) // width == _iota2((n, n // width), 1)).astype(F32)


def _spread(x, seg):
    return lax.dot_general(x, seg, NT_DIMS, precision=lax.Precision.HIGHEST, preferred_element_type=F32)


def _mlstm_decode_kernel(q_ref, k_ref, v_ref, o_ref, gt_ref, bg_ref, gn_ref, c0_ref, n0_ref, m0_ref,
                         h_ref, c_ref, n_ref, m_ref, inter_sc):
    tb = q_ref.shape[0]
    ig, lf = _mlstm_gates(gt_ref[...], bg_ref[...])
    q, k = q_ref[...], k_ref[...] * (DQK_A ** -0.5)
    n_old = n0_ref[...]
    seg_qk, seg_v = _segments(NQK_A, DQK_A), _segments(NV_A, DV_A)
    a = lf + m0_ref[...]
    mt = jnp.maximum(a, ig)
    w_int = jnp.exp(a - mt)
    ws = jnp.exp(ig - mt)
    w = ws * _dot_exact(q * k, seg_qk)
    den = w_int * _dot_exact(q * n_old, seg_qk) + w
    k_w = _spread(ws, seg_qk) * k
    n_ref[...] = _spread(w_int, seg_qk) * n_old + k_w
    m_ref[...] = mt
    q16, kw16, v16 = q.astype(BF16), k_w.astype(BF16), v_ref[...].astype(BF16)
    seq = _iota2((tb, 1), 0)
    for j in range(tb):
        kw_j = jnp.where(seq == j, kw16, jnp.zeros_like(kw16))
        for h in range(NH_A):
            qs, vs = slice(h * DQK_A, (h + 1) * DQK_A), slice(h * DV_A, (h + 1) * DV_A)
            c_old = c0_ref[j, h]
            inter_sc[j:j + 1, vs] = _dot(q16[j:j + 1, qs], c_old.astype(BF16))
            outer = lax.dot_general(kw_j[:, qs], v16[:, vs], TN_DIMS, preferred_element_type=F32)
            c_ref[j, h] = w_int[j:j + 1, h:h + 1] * c_old + outer
    num = _spread(w_int, seg_v) * inter_sc[...] + _spread(w, seg_v) * v_ref[...]
    hc = num / _spread(jnp.maximum(jnp.abs(den), jnp.exp(-mt)), seg_v)
    inv_rms = lax.rsqrt(_dot_exact(hc * hc, seg_v) * (1.0 / DV_A) + EPS)
    h_ref[...] = hc * _spread(inv_rms, seg_v) * gn_ref[...] * _sigmoid(o_ref[...])


def _mlstm_decode(q, k, v, o, gates, b_gates, g_norm, c0, n0, m0):
    nb = q.shape[0]
    tb = MLSTM_DEC_TILE
    rows = lambda i: (i, 0)
    fixed = lambda i: (0, 0)
    return pl.pallas_call(
        _mlstm_decode_kernel,
        grid=(nb // tb,),
        in_specs=[
            pl.BlockSpec((tb, NQK_A), rows), pl.BlockSpec((tb, NQK_A), rows), pl.BlockSpec((tb, NV_A), rows),
            pl.BlockSpec((tb, NV_A), rows), pl.BlockSpec((tb, 2 * NH_A), rows),
            pl.BlockSpec((1, 2 * NH_A), fixed), pl.BlockSpec((1, NV_A), fixed),
            pl.BlockSpec((tb, NH_A, DQK_A, DV_A), lambda i: (i, 0, 0, 0)),
            pl.BlockSpec((tb, NQK_A), rows),
            pl.BlockSpec((tb, NH_A), rows),
        ],
        out_specs=[
            pl.BlockSpec((tb, NV_A), rows),
            pl.BlockSpec((tb, NH_A, DQK_A, DV_A), lambda i: (i, 0, 0, 0)),
            pl.BlockSpec((tb, NQK_A), rows),
            pl.BlockSpec((tb, NH_A), rows),
        ],
        out_shape=[
            jax.ShapeDtypeStruct((nb, NV_A), F32),
            jax.ShapeDtypeStruct(c0.shape, F32),
            jax.ShapeDtypeStruct(n0.shape, F32),
            jax.ShapeDtypeStruct(m0.shape, F32),
        ],
        scratch_shapes=[pltpu.VMEM((tb, NV_A), F32)],
        compiler_params=_params(("parallel",), 32), name="mlstm_decode",
    )(q, k, v, o, gates, b_gates, g_norm, c0, n0, m0)


def _fox_gates_kernel(fg_ref, bf_ref, lft_ref, qb_ref, kb_ref, d_sc):
    t = fg_ref.shape[0]
    lf = _log_sigmoid(fg_ref[...] + bf_ref[...])
    lft_ref[0] = lf.T
    tril = (_iota2((LANES, LANES), 0) >= _iota2((LANES, LANES), 1)).astype(F32)
    carry = jnp.zeros((1, N_HEADS), F32)
    for blk in range(t // LANES):
        rows = slice(blk * LANES, (blk + 1) * LANES)
        d = _dot_exact(tril, lf[rows]) + carry
        d_sc[rows, :] = d
        carry = d[LANES - 1:LANES, :]
    parts = _split3(d_sc[...] * LOG2E)
    head, lane = _iota2((N_HEADS, LANES), 0), _iota2((N_HEADS, LANES), 1)
    lane1 = _iota2((1, LANES), 1) % BIAS_LANES
    qb = jnp.where((lane1 >= 3) & (lane1 < 6), 1.0, 0.0)
    kb = jnp.where(lane1 < 3, 1.0, 0.0)
    for c, part in enumerate(parts):
        qb = qb + _dot(part, jnp.where(lane == head * BIAS_LANES + c, 1.0, 0.0).astype(BF16))
        kb = kb + _dot(part, jnp.where(lane == head * BIAS_LANES + 3 + c, -1.0, 0.0).astype(BF16))
    qb_ref[...] = qb.astype(BF16)
    kb_ref[...] = kb.astype(BF16)


def _fox_gates(fg, b_f, batch):
    m = fg.shape[0]
    t = m // batch
    return pl.pallas_call(
        _fox_gates_kernel,
        grid=(batch,),
        in_specs=[pl.BlockSpec((t, N_HEADS), lambda b: (b, 0)), pl.BlockSpec((1, N_HEADS), lambda b: (0, 0))],
        out_specs=[pl.BlockSpec((1, N_HEADS, t), lambda b: (b, 0, 0)),
                   pl.BlockSpec((t, LANES), lambda b: (b, 0)), pl.BlockSpec((t, LANES), lambda b: (b, 0))],
        out_shape=[jax.ShapeDtypeStruct((batch, N_HEADS, t), F32),
                   jax.ShapeDtypeStruct((m, LANES), BF16), jax.ShapeDtypeStruct((m, LANES), BF16)],
        scratch_shapes=[pltpu.VMEM((t, N_HEADS), F32)],
        compiler_params=_params(("parallel",), 32), name="fox_gates",
    )(fg, b_f)


def _head_lane_mask(hh):
    lane = _iota2((1, LANES), 1)
    return jnp.where((lane >= hh * D_HEAD) & (lane < (hh + 1) * D_HEAD), 1.0, 0.0).astype(BF16)


def _fox_prompt_kernel(q_ref, k_ref, vt_ref, qb_ref, kb_ref, o_ref, m_sc, acc_sc):
    qi, ki = pl.program_id(1), pl.program_id(2)
    tq, tk = q_ref.shape[0], k_ref.shape[0]
    n_pairs = N_HEADS // HEADS_PER_PAIR

    @pl.when(ki == 0)
    def _():
        m_sc[...] = jnp.full_like(m_sc, -jnp.inf)
        acc_sc[...] = jnp.zeros_like(acc_sc)

    def block(diagonal):
        qb, kb = qb_ref[...], kb_ref[...]
        ones = jnp.ones((D_HEAD, tk), BF16)
        bias_head = _iota2((1, LANES), 1) // BIAS_LANES
        visible = _iota2((tk, tq), 0) <= _iota2((tk, tq), 1)

        def pair(hp, carry):
            off = pl.multiple_of(hp * LANES, LANES)
            rhs = jnp.concatenate([q_ref[:, pl.ds(off, LANES)], qb], axis=1)
            kp = k_ref[:, pl.ds(off, LANES)]
            vtp = vt_ref[0, pl.ds(off, LANES), :]
            for hh in range(HEADS_PER_PAIR):
                head = hp * HEADS_PER_PAIR + hh
                kb_h = kb * jnp.where(bias_head == head, 1.0, 0.0).astype(BF16)
                s = _dot_nt(jnp.concatenate([kp * _head_lane_mask(hh), kb_h], axis=1), rhs)
                if diagonal:
                    s = jnp.where(visible, s, -jnp.inf)
                m_old = m_sc[head]
                m_new = jnp.maximum(m_old, jnp.max(s, axis=0, keepdims=True))
                p = jnp.exp2(s - m_new).astype(BF16)
                v_aug = (jnp.concatenate([vtp[:D_HEAD], ones], axis=0) if hh == 0
                         else jnp.concatenate([ones, vtp[D_HEAD:]], axis=0))
                acc_sc[head] = jnp.exp2(m_old - m_new) * acc_sc[head] + _dot(v_aug, p)
                m_sc[head] = m_new
            return carry

        lax.fori_loop(0, n_pairs, pair, 0, unroll=PAIR_UNROLL)

    @pl.when(ki < qi)
    def _():
        block(False)

    @pl.when(ki == qi)
    def _():
        block(True)

    @pl.when(ki == pl.num_programs(2) - 1)
    def _():
        def pair(hp, carry):
            off = pl.multiple_of(hp * LANES, LANES)
            a0, a1 = acc_sc[hp * HEADS_PER_PAIR], acc_sc[hp * HEADS_PER_PAIR + 1]
            o_t = jnp.concatenate([a0[:D_HEAD] * (1.0 / a0[D_HEAD:D_HEAD + 1]),
                                   a1[D_HEAD:] * (1.0 / a1[:1])], axis=0)
            o_ref[:, pl.ds(off, LANES)] = o_t.T.astype(o_ref.dtype)
            return carry

        lax.fori_loop(0, n_pairs, pair, 0)


def _fox_prompt(q, k, vt, qb, kb, batch):
    m = q.shape[0]
    t = m // batch
    tile = ATT_TILE
    n = t // tile
    qmap = lambda b, qi, ki: (b * n + qi, 0)
    kmap = lambda b, qi, ki: (b * n + jnp.minimum(ki, qi), 0)
    return pl.pallas_call(
        _fox_prompt_kernel,
        grid=(batch, n, n),
        in_specs=[
            pl.BlockSpec((tile, D_MODEL), qmap), pl.BlockSpec((tile, D_MODEL), kmap),
            pl.BlockSpec((1, D_MODEL, tile), lambda b, qi, ki: (b, 0, jnp.minimum(ki, qi))),
            pl.BlockSpec((tile, LANES), qmap), pl.BlockSpec((tile, LANES), kmap),
        ],
        out_specs=pl.BlockSpec((tile, D_MODEL), qmap),
        out_shape=jax.ShapeDtypeStruct((m, D_MODEL), BF16),
        scratch_shapes=[pltpu.VMEM((N_HEADS, 1, tile), F32), pltpu.VMEM((N_HEADS, LANES, tile), F32)],
        compiler_params=_params(("parallel", "parallel", "arbitrary"), 40), name="fox_prompt",
    )(q, k, vt, qb, kb)


SUFFIX_ROWS = SB_SUB + 16


def _suffix_lhs():
    r, c = _iota2((SUFFIX_ROWS, SB_SUB), 0), _iota2((SUFFIX_ROWS, SB_SUB), 1)
    u = jnp.where((c > r) | (r >= SB_SUB), 1.0, 0.0).astype(BF16)
    return jnp.concatenate([u, u], axis=1)


def _sb_prompt_kernel(q_ref, k_ref, vt_ref, o_ref, carry_sc, acc_sc):
    qi, kk = pl.program_id(1), pl.program_id(2)
    tq, tk = q_ref.shape[0], k_ref.shape[0]
    n_pairs = N_HEADS // HEADS_PER_PAIR
    n_sub = tk // SB_SUB

    @pl.when(kk == 0)
    def _():
        carry_sc[...] = jnp.zeros_like(carry_sc)
        acc_sc[...] = jnp.zeros_like(acc_sc)

    def neg_log2_one_minus_beta(z):
        return jnp.maximum(z, 0.0) + jnp.log2(1.0 + jnp.exp2(-jnp.abs(z)))

    def suffix_sums(suffix_lhs, nl):
        hi = nl.astype(BF16)
        lo = (nl - hi.astype(F32)).astype(BF16)
        return _dot(suffix_lhs, jnp.concatenate([hi, lo], axis=0))

    def tile(suffix_lhs, km, qp, v_h, newer, visible):
        z = _dot_nt(km, qp)
        nl = neg_log2_one_minus_beta(z)
        if visible is not None:
            nl = jnp.where(visible, nl, 0.0)
        later = [None] * n_sub
        for sub in reversed(range(n_sub)):
            sums = suffix_sums(suffix_lhs, nl[sub * SB_SUB:(sub + 1) * SB_SUB])
            later[sub] = sums[:SB_SUB] + newer
            newer = newer + sums[SB_SUB:SB_SUB + 1]
        a = jnp.exp2(z - nl - jnp.concatenate(later, axis=0))
        if visible is not None:
            a = jnp.where(visible, a, 0.0)
        return _dot(v_h, a.astype(BF16)), newer

    def block(diagonal):
        suffix_lhs = _suffix_lhs()
        zeros = jnp.zeros((D_HEAD, tk), BF16)
        visible = (_iota2((tk, tq), 0) < _iota2((tk, tq), 1)) if diagonal else None

        def pair(hp, carry):
            off = pl.multiple_of(hp * LANES, LANES)
            qp = q_ref[:, pl.ds(off, LANES)]
            kp = k_ref[:, pl.ds(off, LANES)]
            vtp = vt_ref[0, pl.ds(off, LANES), :]
            for hh in range(HEADS_PER_PAIR):
                head = hp * HEADS_PER_PAIR + hh
                v_h = (jnp.concatenate([vtp[:D_HEAD], zeros], axis=0) if hh == 0
                       else jnp.concatenate([zeros, vtp[D_HEAD:]], axis=0))
                update, newer = tile(suffix_lhs, kp * _head_lane_mask(hh), qp, v_h, carry_sc[head], visible)
                acc_sc[hp] += update
                carry_sc[head] = newer
            return carry

        lax.fori_loop(0, n_pairs, pair, 0, unroll=PAIR_UNROLL)

    @pl.when(kk == 0)
    def _():
        block(True)

    @pl.when((kk > 0) & (kk <= qi))
    def _():
        block(False)

    @pl.when(kk == pl.num_programs(2) - 1)
    def _():
        def pair(hp, carry):
            off = pl.multiple_of(hp * LANES, LANES)
            o_ref[:, pl.ds(off, LANES)] = acc_sc[hp].T.astype(o_ref.dtype)
            return carry

        lax.fori_loop(0, n_pairs, pair, 0)


def _sb_prompt(q, k, vt, batch):
    m = q.shape[0]
    t = m // batch
    tile = ATT_TILE
    n = t // tile
    qmap = lambda b, qi, kk: (b * n + qi, 0)
    return pl.pallas_call(
        _sb_prompt_kernel,
        grid=(batch, n, n),
        in_specs=[
            pl.BlockSpec((tile, D_MODEL), qmap),
            pl.BlockSpec((tile, D_MODEL), lambda b, qi, kk: (b * n + jnp.maximum(qi - kk, 0), 0)),
            pl.BlockSpec((1, D_MODEL, tile), lambda b, qi, kk: (b, 0, jnp.maximum(qi - kk, 0))),
        ],
        out_specs=pl.BlockSpec((tile, D_MODEL), qmap),
        out_shape=jax.ShapeDtypeStruct((m, D_MODEL), BF16),
        scratch_shapes=[pltpu.VMEM((N_HEADS, 1, tile), F32),
                        pltpu.VMEM((N_HEADS // HEADS_PER_PAIR, LANES, tile), F32)],
        compiler_params=_params(("parallel", "parallel", "arbitrary"), 40), name="sb_prompt",
    )(q, k, vt)


def _page_spec(block, n_pages, slot):
    p = PAGES_PER_STEP
    return pl.BlockSpec(block, lambda b, j, pt: (pt[b, n_pages - 1 - (j * p + slot)],) + (0,) * (len(block) - 1))


def _block_diag_mask():
    return _iota2((N_HEADS, D_MODEL), 1) // D_HEAD == _iota2((N_HEADS, D_MODEL), 0)


def _page_matrix(page_ref):
    return page_ref[0].reshape(D_MODEL, PAGE_SIZE).astype(BF16)


def _page_scores(k_refs, qbd_sc):
    q = qbd_sc[...]
    return jnp.concatenate([_dot(q, _page_matrix(k_ref)) for k_ref in k_refs], axis=1)


def _page_suffix_sums(x, carry):
    p = x.shape[1] // PAGE_SIZE
    r, c = _iota2((PAGE_SIZE, 2 * PAGE_SIZE), 0), _iota2((PAGE_SIZE, 2 * PAGE_SIZE), 1)
    u = jnp.where((r > c) | (c >= PAGE_SIZE), 1.0, 0.0).astype(BF16)
    stacked = jnp.concatenate([x[:, i * PAGE_SIZE:(i + 1) * PAGE_SIZE] for i in range(p)], axis=0)
    sums = sum(_dot(part, u) for part in _split3(stacked))
    out = []
    for i in range(p):
        rows = slice(i * N_HEADS, (i + 1) * N_HEADS)
        out.append(sums[rows, :PAGE_SIZE] + carry)
        carry = carry + sums[rows, PAGE_SIZE:]
    return jnp.concatenate(out, axis=1), carry


def _accumulate_values(v_refs, w, acc_sc, rescale):
    acc = acc_sc[...]
    if rescale is not None:
        acc = acc * rescale
    for i, v_ref in enumerate(v_refs):
        acc = acc + _dot_nt(w[:, i * PAGE_SIZE:(i + 1) * PAGE_SIZE].astype(BF16), _page_matrix(v_ref))
    acc_sc[...] = acc


def _store_head_rows(acc, o_ref):
    o_ref[0] = jnp.sum(jnp.where(_block_diag_mask(), acc, 0.0), axis=0, keepdims=True)


def _fox_decode_kernel(pt_ref, q_ref, kn_ref, vn_ref, lfn_ref, *refs):
    p = PAGES_PER_STEP
    k_refs, v_refs, lf_refs = refs[:p], refs[p:2 * p], refs[2 * p:3 * p]
    o_ref, qbd_sc, m_sc, l_sc, carry_sc, acc_sc = refs[3 * p:]
    j = pl.program_id(1)

    @pl.when(j == 0)
    def _():
        qbd = jnp.where(_block_diag_mask(), q_ref[0] * (D_HEAD ** -0.5), 0.0)
        qbd_sc[...] = qbd.astype(BF16)
        m_sc[...] = jnp.sum(qbd * kn_ref[0], axis=1, keepdims=True)
        l_sc[...] = jnp.ones_like(l_sc)
        acc_sc[...] = jnp.broadcast_to(vn_ref[0], acc_sc.shape)
        carry_sc[...] = jnp.broadcast_to(lfn_ref[0].T, carry_sc.shape)

    logf = jnp.concatenate([lf_ref[0] for lf_ref in lf_refs], axis=1)
    bias, carry = _page_suffix_sums(logf, carry_sc[...])
    carry_sc[...] = carry
    s = _page_scores(k_refs, qbd_sc) + bias
    m_old = m_sc[...]
    m_new = jnp.maximum(m_old, jnp.max(s, axis=1, keepdims=True))
    alpha = jnp.exp(m_old - m_new)
    w = jnp.exp(s - m_new)
    l_sc[...] = alpha * l_sc[...] + jnp.sum(w, axis=1, keepdims=True)
    m_sc[...] = m_new
    _accumulate_values(v_refs, w, acc_sc, alpha)

    @pl.when(j == pl.num_programs(1) - 1)
    def _():
        _store_head_rows(acc_sc[...] * (1.0 / l_sc[...]), o_ref)


def _decode_scratch():
    return [pltpu.VMEM((N_HEADS, D_MODEL), BF16)]


def _fox_decode(q, k_new, v_new, lf_new, cache_k, cache_v, cache_lf, page_table):
    nb, n_pages = page_table.shape
    p = PAGES_PER_STEP
    row = lambda b, j, pt: (b, 0, 0)
    in_specs = [pl.BlockSpec((1, 1, D_MODEL), row)] * 3 + [pl.BlockSpec((1, 1, N_HEADS), row)]
    in_specs += [_page_spec((1, N_HEADS, D_HEAD, PAGE_SIZE), n_pages, i) for i in range(p)] * 2
    in_specs += [_page_spec((1, N_HEADS, PAGE_SIZE), n_pages, i) for i in range(p)]
    grid_spec = pltpu.PrefetchScalarGridSpec(
        num_scalar_prefetch=1, grid=(nb, n_pages // p), in_specs=in_specs,
        out_specs=pl.BlockSpec((1, 1, D_MODEL), row),
        scratch_shapes=_decode_scratch() + [
            pltpu.VMEM((N_HEADS, 1), F32), pltpu.VMEM((N_HEADS, 1), F32), pltpu.VMEM((N_HEADS, PAGE_SIZE), F32),
            pltpu.VMEM((N_HEADS, D_MODEL), F32)])
    return pl.pallas_call(
        _fox_decode_kernel, grid_spec=grid_spec,
        out_shape=jax.ShapeDtypeStruct((nb, 1, D_MODEL), F32),
        compiler_params=_params(("parallel", "arbitrary"), 40), name="fox_decode",
    )(page_table, q, k_new, v_new, lf_new, *([cache_k] * p), *([cache_v] * p), *([cache_lf] * p))


def _sb_decode_kernel(pt_ref, q_ref, *refs):
    p = PAGES_PER_STEP
    k_refs, v_refs = refs[:p], refs[p:2 * p]
    o_ref, qbd_sc, carry_sc, acc_sc = refs[2 * p:]
    j = pl.program_id(1)

    @pl.when(j == 0)
    def _():
        qbd_sc[...] = jnp.where(_block_diag_mask(), q_ref[0] * (D_HEAD ** -0.5), 0.0).astype(BF16)
        carry_sc[...] = jnp.zeros_like(carry_sc)
        acc_sc[...] = jnp.zeros_like(acc_sc)

    z = _page_scores(k_refs, qbd_sc)
    nl = _softplus(z)
    later, carry = _page_suffix_sums(nl, carry_sc[...])
    carry_sc[...] = carry
    _accumulate_values(v_refs, jnp.exp(z - nl - later), acc_sc, None)

    @pl.when(j == pl.num_programs(1) - 1)
    def _():
        _store_head_rows(acc_sc[...], o_ref)


def _sb_decode(q, cache_k, cache_v, page_table):
    nb, n_pages = page_table.shape
    p = PAGES_PER_STEP
    row = lambda b, j, pt: (b, 0, 0)
    in_specs = [pl.BlockSpec((1, 1, D_MODEL), row)]
    in_specs += [_page_spec((1, N_HEADS, D_HEAD, PAGE_SIZE), n_pages, i) for i in range(p)] * 2
    grid_spec = pltpu.PrefetchScalarGridSpec(
        num_scalar_prefetch=1, grid=(nb, n_pages // p), in_specs=in_specs,
        out_specs=pl.BlockSpec((1, 1, D_MODEL), row),
        scratch_shapes=_decode_scratch() + [
            pltpu.VMEM((N_HEADS, PAGE_SIZE), F32), pltpu.VMEM((N_HEADS, D_MODEL), F32)])
    return pl.pallas_call(
        _sb_decode_kernel, grid_spec=grid_spec,
        out_shape=jax.ShapeDtypeStruct((nb, 1, D_MODEL), F32),
        compiler_params=_params(("parallel", "arbitrary"), 40), name="sb_decode",
    )(page_table, q, *([cache_k] * p), *([cache_v] * p))


def _fox_logf_kernel(fg_ref, bf_ref, lf_ref):
    lf_ref[...] = _log_sigmoid(fg_ref[...] + bf_ref[...])


def _fox_logf(fg, b_f):
    return pl.pallas_call(_fox_logf_kernel, out_shape=jax.ShapeDtypeStruct(fg.shape, F32), name="fox_logf")(fg, b_f)


def _heads_last(x_t, batch, seq):
    return x_t.reshape(batch, N_HEADS, D_HEAD, seq).transpose(0, 3, 1, 2)


def _pages_token_minor(cache):
    return cache.transpose(0, 2, 3, 1)


def kernel(x_prompt, x_sample, state_mlstm_c, state_mlstm_n, state_mlstm_m, cache_fox_k, cache_fox_v, cache_fox_logf, cache_sb_k, cache_sb_v, page_table, norm_mix, norm_ffn, norm_final, mlstm_w_in, mlstm_b_gates, mlstm_norm, mlstm_w_out, fox_w_in, fox_b_f, fox_w_out, sb_w_in, sb_w_out, ffn_w_gu, ffn_w_down):
    batch, seq, d = x_prompt.shape
    nb = x_sample.shape[0]
    xp = x_prompt.reshape(batch * seq, d)
    xs = x_sample.reshape(nb, d)
    q_scale = D_HEAD ** -0.5 * LOG2E
    w_gu, w_down = ffn_w_gu.astype(BF16), ffn_w_down.astype(BF16)
    out = {name: [] for name in ("pmc", "pmn", "pmm", "smc", "smn", "smm", "pfk", "pfv", "pfl", "sfk", "sfv", "sfl",
                                 "psk", "psv", "ssk", "ssv")}
    for i in range(DEPTH):
        j = i // N_MIXERS
        g_mix = norm_mix[i].reshape(1, d)
        if i % N_MIXERS == 0:
            w_in = mlstm_w_in[j].astype(BF16)
            splits = (NQK_A, 2 * NQK_A, 2 * NQK_A + NV_A, 2 * NQK_A + 2 * NV_A)
            ws = [w_in[:, a:b] for a, b in zip((0,) + splits, splits + (w_in.shape[1],))]
            b_gates = mlstm_b_gates[j].reshape(1, 2 * NH_A)
            g_norm = mlstm_norm[j].reshape(1, NV_A)
            w_out = mlstm_w_out[j].astype(BF16)
            plan = ((True, 1.0, (BF16,)), (False, 1.0, (BF16,)), (True, 1.0, (BF16,)), (True, 1.0, (F32,)),
                    (True, 1.0, (F32,)))
            qt, k, vt, ot, gt_t = _proj(xp, g_mix, [ws[0].T, ws[1], ws[2].T, ws[3].T, ws[4].T], plan, batch)
            ap, c1, n1, m1 = _mlstm_prompt(qt, k, vt, ot, gt_t, b_gates.reshape(2 * NH_A, 1),
                                           jnp.broadcast_to(g_norm.reshape(NV_A, 1), (NV_A, MLSTM_TILE)), batch)
            q, k, v, o, gt = _proj(xs, g_mix, ws, ((False, 1.0, (F32,)),) * 5, nb)
            a_s, c2, n2, m2 = _mlstm_decode(q, k, v, o, gt, b_gates, g_norm,
                                            state_mlstm_c[j], state_mlstm_n[j].reshape(nb, NQK_A), state_mlstm_m[j])
            out["pmc"].append(c1); out["pmn"].append(n1); out["pmm"].append(m1.reshape(batch, NH_A))
            out["smc"].append(c2); out["smn"].append(n2.reshape(nb, NH_A, DQK_A)); out["smm"].append(m2)
        elif i % N_MIXERS == 1:
            w_in = fox_w_in[j].astype(BF16)
            wq, wk, wv, wf = (w_in[:, a:b] for a, b in ((0, d), (d, 2 * d), (2 * d, 3 * d), (3 * d, 3 * d + N_HEADS)))
            b_f = fox_b_f[j].reshape(1, N_HEADS)
            w_out = fox_w_out[j].astype(BF16)
            plan = ((False, q_scale, (BF16,)), (False, 1.0, (BF16,)), (True, 1.0, (F32,)), (True, 1.0, (F32, BF16)),
                    (False, 1.0, (F32,)))
            q, kb16, k1t, v1t, vt16, fg = _proj(xp, g_mix, [wq, wk, wk.T, wv.T, wf], plan, batch)
            l1t, qbias, kbias = _fox_gates(fg, b_f, batch)
            ap = _fox_prompt(q, kb16, vt16, qbias, kbias, batch)
            q, k2, v2, fg = _proj(xs, g_mix, [wq, wk, wv, wf], ((False, 1.0, (F32,)),) * 4, nb)
            l2 = _fox_logf(fg, b_f)
            a_s = _fox_decode(q.reshape(nb, 1, d), k2.reshape(nb, 1, d), v2.reshape(nb, 1, d),
                              l2.reshape(nb, 1, N_HEADS),
                              _pages_token_minor(cache_fox_k[j]), _pages_token_minor(cache_fox_v[j]),
                              cache_fox_logf[j].transpose(0, 2, 1), page_table).reshape(nb, d)
            out["pfk"].append(_heads_last(k1t, batch, seq)); out["pfv"].append(_heads_last(v1t, batch, seq))
            out["pfl"].append(l1t.transpose(0, 2, 1))
            out["sfk"].append(k2.reshape(nb, 1, N_HEADS, D_HEAD)); out["sfv"].append(v2.reshape(nb, 1, N_HEADS, D_HEAD))
            out["sfl"].append(l2.reshape(nb, 1, N_HEADS))
        else:
            w_in = sb_w_in[j].astype(BF16)
            wq, wk, wv = (w_in[:, a:b] for a, b in ((0, d), (d, 2 * d), (2 * d, 3 * d)))
            w_out = sb_w_out[j].astype(BF16)
            plan = ((False, q_scale, (BF16,)), (False, 1.0, (BF16,)), (True, 1.0, (F32,)), (True, 1.0, (F32, BF16)))
            q, kb16, k1t, v1t, vt16 = _proj(xp, g_mix, [wq, wk, wk.T, wv.T], plan, batch)
            ap = _sb_prompt(q, kb16, vt16, batch)
            q, k2, v2 = _proj(xs, g_mix, [wq, wk, wv], ((False, 1.0, (F32,)),) * 3, nb)
            a_s = _sb_decode(q.reshape(nb, 1, d), _pages_token_minor(cache_sb_k[j]), _pages_token_minor(cache_sb_v[j]),
                             page_table).reshape(nb, d)
            out["psk"].append(_heads_last(k1t, batch, seq)); out["psv"].append(_heads_last(v1t, batch, seq))
            out["ssk"].append(k2.reshape(nb, 1, N_HEADS, D_HEAD)); out["ssv"].append(v2.reshape(nb, 1, N_HEADS, D_HEAD))
        g_ffn = norm_ffn[i].reshape(1, d)
        g_fin = norm_final.reshape(1, d) if i == DEPTH - 1 else None
        xp = _out_ffn(xp, ap, w_out, g_ffn, w_gu, w_down, i, g_fin)
        xs = _out_ffn(xs, a_s, w_out, g_ffn, w_gu, w_down, i, g_fin)
    y_prompt = xp.reshape(batch, seq, d)
    y_sample = xs.reshape(nb, 1, d)
    st = {name: jnp.stack(vals) for name, vals in out.items()}
    return (y_prompt, y_sample, st["pmc"], st["pmn"], st["pmm"], st["smc"], st["smn"], st["smm"],
            st["pfk"], st["pfv"], st["pfl"], st["sfk"], st["sfv"], st["sfl"],
            st["psk"], st["psv"], st["ssk"], st["ssv"])
```
